```python
import math
import jax, jax.numpy as jnp
from jax import lax
import numpy as np

D_MODEL = 1024
BATCH = 8
SEQ = 4096
DEPTH = 1

HEAD_DIM = 64
BLOCK = 128
A_PAIRS = ((128, 1), (512, 4), (2048, 16))
A_GROUPS = 3
A_HEADS_PER_GROUP = 4
A_WIDTH = A_GROUPS * A_HEADS_PER_GROUP * HEAD_DIM
A_OUT_WIDTH = A_HEADS_PER_GROUP * HEAD_DIM
B_Q_HEADS = 8
B_KV_HEADS = 2
B_WINDOW = 128
B_Q_WIDTH = B_Q_HEADS * HEAD_DIM
B_KV_WIDTH = B_KV_HEADS * HEAD_DIM
C_HEADS = 4
C_HEAD_DIM = 128
C_WIDTH = C_HEADS * C_HEAD_DIM
MEM_LEN = 256
N_GATES = 3
IN_WIDTH = 3 * A_WIDTH + B_Q_WIDTH + 2 * B_KV_WIDTH + C_WIDTH + N_GATES * D_MODEL
N_BUCKETS = 32
MAX_DISTANCE = 2048
N_BIAS_HEADS = A_GROUPS * A_HEADS_PER_GROUP + B_Q_HEADS
PEER_HEADS = 8
N_KEYS = 128
N_EXPERTS = N_KEYS * N_KEYS
PEER_TOPK = 16
PEER_DKEY = 128
PEER_CHUNK = 128
ALPHA = (2.0 * DEPTH) ** 0.25
BETA = (8.0 * DEPTH) ** -0.25
LN_EPS = 1e-5
NEG = -1e30

kernel_name = "hybrid_dilated_swa_sink_mem_peer_deepnorm"


def _layer_norm(x, g, b):
    xf = x.astype(jnp.float32)
    mu = jnp.mean(xf, axis=-1, keepdims=True)
    xc = xf - mu
    var = jnp.mean(xc * xc, axis=-1, keepdims=True)
    y = xc * lax.rsqrt(var + LN_EPS) * g.astype(jnp.float32) + b.astype(jnp.float32)
    return y.astype(x.dtype)


def _t5_bucket(dist):
    n = np.asarray(dist, dtype=np.int32)
    max_exact = N_BUCKETS // 2
    nf = np.maximum(n, 1).astype(np.float32)
    scale = np.float32(math.log(MAX_DISTANCE / max_exact))
    large = max_exact + (np.log(nf / np.float32(max_exact)) / scale
                         * np.float32(N_BUCKETS - max_exact)).astype(np.int32)
    large = np.minimum(large, N_BUCKETS - 1)
    return np.where(n < max_exact, n, large).astype(np.int32)


def _with_prev(t, axis):
    pad = [(0, 0)] * t.ndim
    pad[axis] = (1, 0)
    prev = lax.slice_in_dim(jnp.pad(t, pad), 0, t.shape[axis], axis=axis)
    return jnp.concatenate([prev, t], axis=axis + 1)


def _dilated_group(q, k, v, table, window, dilation):
    Bn, S, H, dh = q.shape
    r = dilation
    L = S // r
    nblk = -(-L // BLOCK)
    Lp = nblk * BLOCK
    W = window // r

    def strided(t):
        t = t.reshape(Bn, L, r, H, dh).transpose(0, 2, 1, 3, 4)
        t = jnp.pad(t, ((0, 0), (0, 0), (0, Lp - L), (0, 0), (0, 0)))
        return t.reshape(Bn, r, nblk, BLOCK, H, dh)

    qb = strided(q)
    kc = _with_prev(strided(k), axis=2)
    vc = _with_prev(strided(v), axis=2)
    logits = jnp.einsum('brnqhd,brnkhd->brnhqk', qb, kc).astype(jnp.float32) * (dh ** -0.5)

    i = np.arange(BLOCK)[:, None]
    j = np.arange(2 * BLOCK)[None, :]
    off = BLOCK + i - j
    bucket = _t5_bucket(np.clip(off, 0, W) * r)
    bias = jnp.take(table.astype(jnp.float32), bucket, axis=0).transpose(2, 0, 1)
    valid = ((off >= 0) & (off <= W))[None] & ~((np.arange(nblk)[:, None, None] == 0) & (j[None] < BLOCK))
    logits = jnp.where(valid[None, None, :, None], logits + bias[None, None, None], NEG)

    lse = jax.nn.logsumexp(logits, axis=-1)
    p = jnp.exp(logits - lse[..., None]).astype(v.dtype)
    o = jnp.einsum('brnhqk,brnkhd->brnqhd', p, vc)
    o = o.reshape(Bn, r, Lp, H, dh)[:, :, :L].transpose(0, 2, 1, 3, 4).reshape(Bn, S, H, dh)
    lse = lse.transpose(0, 1, 2, 4, 3).reshape(Bn, r, Lp, H)[:, :, :L].transpose(0, 2, 1, 3).reshape(Bn, S, H)
    return o, lse


def _swa_sinks(q, k, v, table, sinks):
    Bn, S, Hq, dh = q.shape
    Hkv = k.shape[2]
    G = Hq // Hkv
    nb = S // BLOCK
    qb = q.reshape(Bn, nb, BLOCK, Hkv, G, dh)
    kc = _with_prev(k.reshape(Bn, nb, BLOCK, Hkv, dh), axis=1)
    vc = _with_prev(v.reshape(Bn, nb, BLOCK, Hkv, dh), axis=1)
    logits = jnp.einsum('bnqkgd,bnjkd->bnkgqj', qb, kc).astype(jnp.float32) * (dh ** -0.5)

    i = np.arange(BLOCK)[:, None]
    j = np.arange(2 * BLOCK)[None, :]
    off = BLOCK + i - j
    bucket = _t5_bucket(np.clip(off, 0, B_WINDOW - 1))
    bias = jnp.take(table.astype(jnp.float32), bucket, axis=0).transpose(2, 0, 1)
    bias = bias.reshape(Hkv, G, BLOCK, 2 * BLOCK)
    valid = ((off >= 0) & (off < B_WINDOW))[None] & ~((np.arange(nb)[:, None, None] == 0) & (j[None] < BLOCK))
    logits = jnp.where(valid[None, :, None, None], logits + bias[None, None], NEG)

    s = sinks.astype(jnp.float32).reshape(Hkv, G)[None, None, :, :, None, None]
    m = jnp.maximum(jnp.max(logits, axis=-1, keepdims=True), s)
    e = jnp.exp(logits - m)
    denom = jnp.sum(e, axis=-1, keepdims=True) + jnp.exp(s - m)
    p = (e / denom).astype(v.dtype)
    o = jnp.einsum('bnkgqj,bnjkd->bnqkgd', p, vc)
    return o.reshape(Bn, S, Hq * dh)


def _memory_attn(q, k_mem, v_mem):
    Bn, S, H, dc = q.shape
    logits = jnp.einsum('bshd,bmhd->bhsm', q, k_mem).astype(jnp.float32) * (dc ** -0.5)
    p = jax.nn.softmax(logits, axis=-1).astype(v_mem.dtype)
    return jnp.einsum('bhsm,bmhd->bshd', p, v_mem).reshape(Bn, S, H * dc)


def _peer(x, w_query, sub_keys, u_table, v_table):
    Bn, S, D = x.shape
    q = (x @ w_query).reshape(Bn, S, PEER_HEADS, 2, PEER_DKEY // 2)
    scores = jnp.einsum('bshcd,hckd->bshck', q, sub_keys).astype(jnp.float32)
    sv, si = lax.top_k(scores, PEER_TOPK)
    cand = (sv[..., 0, :, None] + sv[..., 1, None, :]).reshape(Bn, S, PEER_HEADS, PEER_TOPK * PEER_TOPK)
    cidx = (si[..., 0, :, None] * N_KEYS + si[..., 1, None, :]).reshape(Bn, S, PEER_HEADS, PEER_TOPK * PEER_TOPK)
    top, pos = lax.top_k(cand, PEER_TOPK)
    eidx = jnp.take_along_axis(cidx, pos, axis=-1)
    gate = jax.nn.softmax(top, axis=-1).astype(x.dtype)

    T = Bn * S
    nchunk = T // PEER_CHUNK
    xs = x.reshape(nchunk, PEER_CHUNK, D)
    es = eidx.reshape(nchunk, PEER_CHUNK, PEER_HEADS * PEER_TOPK)
    gs = gate.reshape(nchunk, PEER_CHUNK, PEER_HEADS * PEER_TOPK)

    def expert_chunk(args):
        xc, ec, gc = args
        u = jnp.take(u_table, ec, axis=0)
        h = jax.nn.gelu(jnp.einsum('td,tkd->tk', xc, u), approximate=False)
        vv = jnp.take(v_table, ec, axis=0)
        return jnp.einsum('tk,tkd->td', gc * h, vv)

    return lax.map(expert_chunk, (xs, es, gs)).reshape(Bn, S, D)


def setup_inputs(seed: int = 0) -> dict:
    key = jax.random.key(seed)
    ks = jax.random.split(key, 20)
    f32 = jnp.float32
    D = D_MODEL
    nrm = lambda k, shape, s: (jax.random.normal(k, shape, f32) * s)
    return {
        "x": nrm(ks[0], (BATCH, SEQ, D), 1.0),
        "mem": nrm(ks[1], (BATCH, MEM_LEN, D), 1.0),
        "rel_bias": nrm(ks[2], (N_BUCKETS, N_BIAS_HEADS), 0.5),
        "w_in": nrm(ks[3], (DEPTH, D, IN_WIDTH), D ** -0.5),
        "b_gate": nrm(ks[4], (DEPTH, N_GATES * D), 0.02),
        "w_mem_kv": nrm(ks[5], (DEPTH, D, 2 * C_WIDTH), D ** -0.5),
        "sinks": nrm(ks[6], (DEPTH, B_Q_HEADS), 0.5),
        "w_branch_a": nrm(ks[7], (DEPTH, A_OUT_WIDTH, D), A_OUT_WIDTH ** -0.5),
        "w_branch_b": nrm(ks[8], (DEPTH, B_Q_WIDTH, D), B_Q_WIDTH ** -0.5),
        "w_branch_c": nrm(ks[9], (DEPTH, C_WIDTH, D), C_WIDTH ** -0.5),
        "w_out": nrm(ks[10], (DEPTH, D, D), BETA * D ** -0.5),
        "ln1_g": 1.0 + nrm(ks[11], (DEPTH, D), 0.05),
        "ln1_b": nrm(ks[12], (DEPTH, D), 0.02),
        "peer_w_query": nrm(ks[13], (DEPTH, D, PEER_HEADS * PEER_DKEY), D ** -0.5),
        "peer_sub_keys": nrm(ks[14], (DEPTH, PEER_HEADS, 2, N_KEYS, PEER_DKEY // 2), (PEER_DKEY // 2) ** -0.5),
        "peer_u": nrm(ks[15], (DEPTH, N_EXPERTS, D), D ** -0.5),
        "peer_v": nrm(ks[16], (DEPTH, N_EXPERTS, D), BETA * 0.5),
        "ln2_g": 1.0 + nrm(ks[17], (DEPTH, D), 0.05),
        "ln2_b": nrm(ks[18], (DEPTH, D), 0.02),
    }


def reference(x, mem, rel_bias, w_in, b_gate, w_mem_kv, sinks, w_branch_a, w_branch_b, w_branch_c,
              w_out, ln1_g, ln1_b, peer_w_query, peer_sub_keys, peer_u, peer_v, ln2_g, ln2_b):
    Bn, S, D = x.shape
    M = mem.shape[1]
    widths = [A_WIDTH, A_WIDTH, A_WIDTH, B_Q_WIDTH, B_KV_WIDTH, B_KV_WIDTH, C_WIDTH, N_GATES * D_MODEL]
    split_idx = [int(c) for c in np.cumsum(widths)[:-1]]
    n_a = A_GROUPS * A_HEADS_PER_GROUP
    for l in range(DEPTH):
        h = x @ w_in[l]
        aq, ak, av, bq, bk, bv, cq, gates = jnp.split(h, split_idx, axis=-1)
        aq = aq.reshape(Bn, S, A_GROUPS, A_HEADS_PER_GROUP, HEAD_DIM)
        ak = ak.reshape(Bn, S, A_GROUPS, A_HEADS_PER_GROUP, HEAD_DIM)
        av = av.reshape(Bn, S, A_GROUPS, A_HEADS_PER_GROUP, HEAD_DIM)

        outs, lses = [], []
        for g, (win, dil) in enumerate(A_PAIRS):
            tbl = rel_bias[:, g * A_HEADS_PER_GROUP:(g + 1) * A_HEADS_PER_GROUP]
            o_g, l_g = _dilated_group(aq[:, :, g], ak[:, :, g], av[:, :, g], tbl, win, dil)
            outs.append(o_g)
            lses.append(l_g)
        mix = jax.nn.softmax(jnp.stack(lses, axis=0), axis=0).astype(x.dtype)
        y_a = jnp.sum(mix[..., None] * jnp.stack(outs, axis=0), axis=0).reshape(Bn, S, A_OUT_WIDTH)

        y_b = _swa_sinks(bq.reshape(Bn, S, B_Q_HEADS, HEAD_DIM),
                         bk.reshape(Bn, S, B_KV_HEADS, HEAD_DIM),
                         bv.reshape(Bn, S, B_KV_HEADS, HEAD_DIM),
                         rel_bias[:, n_a:n_a + B_Q_HEADS], sinks[l])

        kv_m = (mem @ w_mem_kv[l]).reshape(Bn, M, 2, C_HEADS, C_HEAD_DIM)
        y_c = _memory_attn(cq.reshape(Bn, S, C_HEADS, C_HEAD_DIM), kv_m[:, :, 0], kv_m[:, :, 1])

        gt = jax.nn.sigmoid((gates.reshape(Bn, S, N_GATES, D) + b_gate[l].reshape(N_GATES, D)).astype(jnp.float32)).astype(x.dtype)
        merged = (gt[:, :, 0] * (y_a @ w_branch_a[l])
                  + gt[:, :, 1] * (y_b @ w_branch_b[l])
                  + gt[:, :, 2] * (y_c @ w_branch_c[l]))
        x = _layer_norm(ALPHA * x + merged @ w_out[l], ln1_g[l], ln1_b[l])

        y_p = _peer(x, peer_w_query[l], peer_sub_keys[l], peer_u[l], peer_v[l])
        x = _layer_norm(ALPHA * x + y_p, ln2_g[l], ln2_b[l])
    return x
```

```python
import functools
import math

import numpy as np
import jax
import jax.numpy as jnp
from jax import lax
from jax.experimental import pallas as pl
from jax.experimental.pallas import tpu as pltpu

F32 = jnp.float32
BF16 = jnp.bfloat16

HEAD_DIM = 64
BLOCK = 128
A_PAIRS = ((128, 1), (512, 4), (2048, 16))
A_GROUPS = 3
A_HEADS = 4
A_GROUP_WIDTH = A_HEADS * HEAD_DIM
A_WIDTH = A_GROUPS * A_GROUP_WIDTH
B_Q_HEADS = 8
B_KV_HEADS = 2
B_WINDOW = 128
B_Q_WIDTH = B_Q_HEADS * HEAD_DIM
B_KV_WIDTH = B_KV_HEADS * HEAD_DIM
C_HEADS = 4
C_HEAD_DIM = 128
C_WIDTH = C_HEADS * C_HEAD_DIM
N_GATES = 3
N_BUCKETS = 32
MAX_DISTANCE = 2048
PEER_HEADS = 8
N_KEYS = 128
PEER_TOPK = 16
PEER_HALF = 64
LN_EPS = 1e-5
NEG = -1e30

V7X_VMEM_BYTES = 64 * 1024 * 1024
SUBLANES = 8
LANES = 128


def _cparams(n_grid, vmem_mb):
    return pltpu.CompilerParams(
        dimension_semantics=("arbitrary",) * n_grid,
        vmem_limit_bytes=vmem_mb * 1024 * 1024,
    )


def _dot(a, b):
    return jnp.dot(a, b, preferred_element_type=F32)


def _dot_nt(a, b):
    return lax.dot_general(a, b, (((1,), (1,)), ((), ())), preferred_element_type=F32)


def _layer_norm(y, g, b):
    mu = jnp.mean(y, axis=-1, keepdims=True)
    yc = y - mu
    var = jnp.mean(yc * yc, axis=-1, keepdims=True)
    return yc * lax.rsqrt(var + LN_EPS) * g + b


def _proj_kernel(x_ref, w_ref, o_ref):
    o_ref[...] = _dot(x_ref[...].astype(BF16), w_ref[...]).astype(o_ref.dtype)


def _project(x, w, tm, tn, name):
    m, k = x.shape
    n = w.shape[1]
    return pl.pallas_call(
        _proj_kernel,
        grid=(n // tn, m // tm),
        in_specs=[pl.BlockSpec((tm, k), lambda j, i: (i, 0)),
                  pl.BlockSpec((k, tn), lambda j, i: (0, j))],
        out_specs=pl.BlockSpec((tm, tn), lambda j, i: (i, j)),
        out_shape=jax.ShapeDtypeStruct((m, n), BF16),
        compiler_params=_cparams(2, 48),
        name=name,
    )(x, w)


def _t5_bucket(dist):
    n = np.asarray(dist, dtype=np.int32)
    max_exact = N_BUCKETS // 2
    nf = np.maximum(n, 1).astype(np.float32)
    scale = np.float32(math.log(MAX_DISTANCE / max_exact))
    large = max_exact + (np.log(nf / np.float32(max_exact)) / scale
                         * np.float32(N_BUCKETS - max_exact)).astype(np.int32)
    large = np.minimum(large, N_BUCKETS - 1)
    return np.where(n < max_exact, n, large).astype(np.int32)


def _bias_tiles(table, dist_scale, max_off):
    i = np.arange(BLOCK)[:, None]
    j = np.arange(2 * BLOCK)[None, :]
    off = BLOCK + i - j
    bucket = _t5_bucket(np.clip(off, 0, max_off) * dist_scale)
    valid = (off >= 0) & (off <= max_off)
    bias = jnp.take(table.astype(F32), bucket, axis=0).transpose(2, 0, 1)
    general = jnp.where(valid[None], bias, NEG)
    first = jnp.where((valid & (j >= BLOCK))[None], bias, NEG)
    return jnp.stack([first, general], axis=0)


def _attn_a_kernel(q_ref, kp_ref, kc_ref, vp_ref, vc_ref, bias_ref, o_ref, l_ref):
    q = q_ref[0]
    kp = kp_ref[0]
    kc = kc_ref[0]
    vp = vp_ref[0]
    vc = vc_ref[0]
    scale = HEAD_DIM ** -0.5
    for h in range(A_HEADS):
        sl = slice(HEAD_DIM * h, HEAD_DIM * (h + 1))
        qh = q[:, sl]
        lp = _dot_nt(qh, kp[:, sl]) * scale + bias_ref[0, h, :, :BLOCK]
        lc = _dot_nt(qh, kc[:, sl]) * scale + bias_ref[0, h, :, BLOCK:]
        m = jnp.maximum(jnp.max(lp, axis=-1, keepdims=True), jnp.max(lc, axis=-1, keepdims=True))
        ep = jnp.exp(lp - m)
        ec = jnp.exp(lc - m)
        s = jnp.sum(ep, axis=-1, keepdims=True) + jnp.sum(ec, axis=-1, keepdims=True)
        inv = 1.0 / s
        o = _dot((ep * inv).astype(BF16), vp[:, sl]) + _dot((ec * inv).astype(BF16), vc[:, sl])
        o_ref[0, :, sl] = o
        l_ref[0, :, sl] = jnp.broadcast_to(m + jnp.log(s), (BLOCK, HEAD_DIM))


def _attn_a_group(h3, bias, g, r, col_q, col_k, col_v, row_w):
    bsz, s, _ = h3.shape
    ln = s // r
    nblk = ln // BLOCK
    hv = h3.reshape(bsz, ln, r * row_w)
    per_row = row_w // A_GROUP_WIDTH
    cq, ck, cv = (c // A_GROUP_WIDTH + g for c in (col_q, col_k, col_v))

    def spec(col, prev):
        if prev:
            return pl.BlockSpec((1, BLOCK, A_GROUP_WIDTH),
                                lambda b, c, n: (b, jnp.maximum(n - 1, 0), per_row * c + col))
        return pl.BlockSpec((1, BLOCK, A_GROUP_WIDTH), lambda b, c, n: (b, n, per_row * c + col))

    out_spec = pl.BlockSpec((1, BLOCK, A_GROUP_WIDTH), lambda b, c, n: (b, n, c))
    o, l = pl.pallas_call(
        _attn_a_kernel,
        grid=(bsz, r, nblk),
        in_specs=[spec(cq, False), spec(ck, True), spec(ck, False), spec(cv, True), spec(cv, False),
                  pl.BlockSpec((1, A_HEADS, BLOCK, 2 * BLOCK),
                               lambda b, c, n: (jnp.minimum(n, 1), 0, 0, 0))],
        out_specs=[out_spec, out_spec],
        out_shape=[jax.ShapeDtypeStruct((bsz, ln, r * A_GROUP_WIDTH), F32)] * 2,
        compiler_params=_cparams(3, 32),
        name=f"attn_a{g}",
    )(hv, hv, hv, hv, hv, bias)
    return o.reshape(bsz, s, A_GROUP_WIDTH), l.reshape(bsz, s, A_GROUP_WIDTH)


def _attn_b_kernel(sinks_ref, q0_ref, q1_ref, kp_ref, kc_ref, vp_ref, vc_ref, bias_ref, o_ref):
    kp = kp_ref[0]
    kc = kc_ref[0]
    vp = vp_ref[0]
    vc = vc_ref[0]
    scale = HEAD_DIM ** -0.5
    per_kv = B_Q_HEADS // B_KV_HEADS
    for h in range(B_Q_HEADS):
        kv = h // per_kv
        q = (q0_ref if kv == 0 else q1_ref)[0]
        qsl = slice(HEAD_DIM * (h % per_kv), HEAD_DIM * (h % per_kv + 1))
        ksl = slice(HEAD_DIM * kv, HEAD_DIM * (kv + 1))
        qh = q[:, qsl]
        lp = _dot_nt(qh, kp[:, ksl]) * scale + bias_ref[0, h, :, :BLOCK]
        lc = _dot_nt(qh, kc[:, ksl]) * scale + bias_ref[0, h, :, BLOCK:]
        sink = sinks_ref[h]
        m = jnp.maximum(jnp.max(lp, axis=-1, keepdims=True), jnp.max(lc, axis=-1, keepdims=True))
        m = jnp.maximum(m, sink)
        ep = jnp.exp(lp - m)
        ec = jnp.exp(lc - m)
        denom = (jnp.sum(ep, axis=-1, keepdims=True) + jnp.sum(ec, axis=-1, keepdims=True)
                 + jnp.exp(sink - m))
        inv = 1.0 / denom
        o = _dot((ep * inv).astype(BF16), vp[:, ksl]) + _dot((ec * inv).astype(BF16), vc[:, ksl])
        o_ref[0, :, HEAD_DIM * h:HEAD_DIM * (h + 1)] = o.astype(o_ref.dtype)


def _attn_b(h3, bias, sinks, col_q, col_k, col_v):
    bsz, s, _ = h3.shape
    nb = s // BLOCK
    qw = B_Q_WIDTH // B_KV_HEADS
    cq = col_q // qw
    ck = col_k // B_KV_WIDTH
    cv = col_v // B_KV_WIDTH

    def kv_spec(col, prev):
        if prev:
            return pl.BlockSpec((1, BLOCK, B_KV_WIDTH), lambda b, n: (b, jnp.maximum(n - 1, 0), col))
        return pl.BlockSpec((1, BLOCK, B_KV_WIDTH), lambda b, n: (b, n, col))

    return pl.pallas_call(
        _attn_b_kernel,
        grid=(bsz, nb),
        in_specs=[pl.BlockSpec(memory_space=pltpu.SMEM),
                  pl.BlockSpec((1, BLOCK, qw), lambda b, n: (b, n, cq)),
                  pl.BlockSpec((1, BLOCK, qw), lambda b, n: (b, n, cq + 1)),
                  kv_spec(ck, True), kv_spec(ck, False), kv_spec(cv, True), kv_spec(cv, False),
                  pl.BlockSpec((1, B_Q_HEADS, BLOCK, 2 * BLOCK),
                               lambda b, n: (jnp.minimum(n, 1), 0, 0, 0))],
        out_specs=pl.BlockSpec((1, BLOCK, B_Q_WIDTH), lambda b, n: (b, n, 0)),
        out_shape=jax.ShapeDtypeStruct((bsz, s, B_Q_WIDTH), BF16),
        compiler_params=_cparams(2, 32),
        name="attn_b",
    )(sinks.astype(F32), h3, h3, h3, h3, h3, h3, bias)


def _attn_c_kernel(q_ref, k_ref, v_ref, o_ref):
    q = q_ref[0]
    k = k_ref[0]
    v = v_ref[0]
    scale = C_HEAD_DIM ** -0.5
    for h in range(C_HEADS):
        sl = slice(C_HEAD_DIM * h, C_HEAD_DIM * (h + 1))
        logits = _dot_nt(q[:, sl], k[:, sl]) * scale
        m = jnp.max(logits, axis=-1, keepdims=True)
        e = jnp.exp(logits - m)
        inv = 1.0 / jnp.sum(e, axis=-1, keepdims=True)
        o_ref[0, :, sl] = _dot((e * inv).astype(BF16), v[:, sl]).astype(o_ref.dtype)


def _attn_c(h3, kvm, col_q, tq):
    bsz, s, _ = h3.shape
    m = kvm.shape[1]
    cq = col_q // C_WIDTH
    return pl.pallas_call(
        _attn_c_kernel,
        grid=(bsz, s // tq),
        in_specs=[pl.BlockSpec((1, tq, C_WIDTH), lambda b, n: (b, n, cq)),
                  pl.BlockSpec((1, m, C_WIDTH), lambda b, n: (b, 0, 0)),
                  pl.BlockSpec((1, m, C_WIDTH), lambda b, n: (b, 0, 1))],
        out_specs=pl.BlockSpec((1, tq, C_WIDTH), lambda b, n: (b, n, 0)),
        out_shape=jax.ShapeDtypeStruct((bsz, s, C_WIDTH), BF16),
        compiler_params=_cparams(2, 32),
        name="attn_c",
    )(h3, kvm, kvm)


def _merge_kernel(alpha, o0, o1, o2, l0, l1, l2, yb_ref, yc_ref, gate_ref, x_ref, bg_ref,
                  wa_ref, wb_ref, wc_ref, wo_ref, g1_ref, b1_ref, wq_ref,
                  x1_ref, x1b_ref, q_ref):
    d = x_ref.shape[-1]
    la, lb, lc = l0[...], l1[...], l2[...]
    m = jnp.maximum(jnp.maximum(la, lb), lc)
    wa, wb, wc = jnp.exp(la - m), jnp.exp(lb - m), jnp.exp(lc - m)
    inv = 1.0 / (wa + wb + wc)
    ya = (wa * inv) * o0[...] + (wb * inv) * o1[...] + (wc * inv) * o2[...]

    def gate(n):
        pre = gate_ref[:, n * d:(n + 1) * d].astype(F32) + bg_ref[:, n * d:(n + 1) * d]
        return jax.nn.sigmoid(pre)

    merged = (gate(0) * _dot(ya.astype(BF16), wa_ref[...])
              + gate(1) * _dot(yb_ref[...], wb_ref[...])
              + gate(2) * _dot(yc_ref[...], wc_ref[...]))
    y = alpha * x_ref[...] + _dot(merged.astype(BF16), wo_ref[...])
    x1 = _layer_norm(y, g1_ref[...], b1_ref[...])
    x1_ref[...] = x1
    x1b = x1.astype(BF16)
    x1b_ref[...] = x1b
    q_ref[...] = _dot(x1b, wq_ref[...]).astype(q_ref.dtype)


def _merge(alpha, oa, la, yb, yc, h2, x2, b_gate, w_a, w_b, w_c, w_o, g1, b1, w_q, tm):
    t, d = x2.shape
    row = lambda w: pl.BlockSpec((tm, w), lambda i: (i, 0))
    full = lambda a: pl.BlockSpec(a.shape, lambda i: (0,) * a.ndim)
    ins = [*oa, *la, yb, yc, h2, x2, b_gate, w_a, w_b, w_c, w_o, g1, b1, w_q]
    in_specs = ([row(A_GROUP_WIDTH)] * 6 + [row(B_Q_WIDTH), row(C_WIDTH), row(N_GATES * d), row(d)]
                + [full(a) for a in ins[10:]])
    return pl.pallas_call(
        functools.partial(_merge_kernel, alpha),
        grid=(t // tm,),
        in_specs=in_specs,
        out_specs=[row(d), row(d), row(w_q.shape[1])],
        out_shape=[jax.ShapeDtypeStruct((t, d), F32), jax.ShapeDtypeStruct((t, d), BF16),
                   jax.ShapeDtypeStruct((t, w_q.shape[1]), BF16)],
        compiler_params=_cparams(1, 48),
        name="merge_ln1",
    )(*ins)


def _extract_top(vals, payloads, k):
    n = vals.shape[0]
    row = lax.broadcasted_iota(jnp.int32, vals.shape, 0)
    tops = []
    picked = [[] for _ in payloads]
    for _ in range(k):
        m = jnp.max(vals, axis=0, keepdims=True)
        first = jnp.min(jnp.where(vals == m, row, n), axis=0, keepdims=True)
        sel = row == first
        tops.append(m)
        for lst, p in zip(picked, payloads):
            lst.append(jnp.max(jnp.where(sel, p, -1), axis=0, keepdims=True))
        vals = jnp.where(sel, -jnp.inf, vals)
    return jnp.concatenate(tops, axis=0), [jnp.concatenate(lst, axis=0) for lst in picked]


def _retrieve_kernel(q_ref, keys_ref, i_ref, j_ref, g_ref):
    tt = q_ref.shape[0]
    q = q_ref[...]
    key_row = lax.broadcasted_iota(jnp.int32, (N_KEYS, tt), 0)
    out_i, out_j, out_g = [], [], []
    for h in range(PEER_HEADS):
        v, ix = [], []
        for c in range(2):
            col = (2 * h + c) * PEER_HALF
            scores = _dot_nt(keys_ref[h, c], q[:, col:col + PEER_HALF])
            tv, (ti,) = _extract_top(scores, [key_row], PEER_TOPK)
            v.append(tv)
            ix.append(ti)
        cv, ci, cj = [], [], []
        for k1 in range(PEER_TOPK):
            n2 = PEER_TOPK // (k1 + 1)
            cv.append(v[0][k1:k1 + 1] + v[1][:n2])
            ci.append(jnp.broadcast_to(ix[0][k1:k1 + 1], (n2, tt)))
            cj.append(ix[1][:n2])
        cv = jnp.concatenate(cv, axis=0)
        ci = jnp.concatenate(ci, axis=0)
        cj = jnp.concatenate(cj, axis=0)
        top, (ei, ej) = _extract_top(cv, [ci, cj], PEER_TOPK)
        e = jnp.exp(top - top[0:1])
        out_g.append(e * (1.0 / jnp.sum(e, axis=0, keepdims=True)))
        out_i.append(ei)
        out_j.append(ej)
    i_ref[...] = jnp.concatenate(out_i, axis=0).T
    j_ref[...] = jnp.concatenate(out_j, axis=0).T
    g_ref[...] = jnp.concatenate(out_g, axis=0).T


def _retrieve(q, keys, tt):
    t, w = q.shape
    npair = PEER_HEADS * PEER_TOPK
    out = pl.BlockSpec((tt, npair), lambda i: (i, 0))
    return pl.pallas_call(
        _retrieve_kernel,
        grid=(t // tt,),
        in_specs=[pl.BlockSpec((tt, w), lambda i: (i, 0)),
                  pl.BlockSpec(keys.shape, lambda i: (0, 0, 0, 0))],
        out_specs=[out, out, out],
        out_shape=[jax.ShapeDtypeStruct((t, npair), jnp.int32),
                   jax.ShapeDtypeStruct((t, npair), jnp.int32),
                   jax.ShapeDtypeStruct((t, npair), F32)],
        compiler_params=_cparams(1, 32),
        name="peer_retrieve",
    )(q, keys)


TOKEN_PITCH = N_KEYS + SUBLANES


def _gate_matrix_kernel(i_ref, j_ref, g_ref, o_ref, tile_ref):
    tt = i_ref.shape[0]
    key = lax.broadcasted_iota(jnp.int32, (N_KEYS, i_ref.shape[1]), 0)

    def per_token(t, carry):
        irow = i_ref[pl.ds(t, 1), :]
        jrow = j_ref[pl.ds(t, 1), :]
        grow = g_ref[pl.ds(t, 1), :]
        a = jnp.where(key == irow, grow, 0.0).astype(BF16)
        b = jnp.where(key == jrow, 1.0, 0.0).astype(BF16)
        base = pl.multiple_of(t * TOKEN_PITCH, SUBLANES)
        tile_ref[pl.ds(base, N_KEYS), :] = _dot_nt(a, b)
        return carry

    lax.fori_loop(0, tt, per_token, 0)
    for i in range(N_KEYS):
        o_ref[:, i * N_KEYS:(i + 1) * N_KEYS] = (
            tile_ref[pl.ds(i, tt, stride=TOKEN_PITCH), :].astype(o_ref.dtype))


def _gate_matrix(ei, ej, gate, tt):
    t, npair = ei.shape
    ne = N_KEYS * N_KEYS
    spec = pl.BlockSpec((tt, npair), lambda i: (i, 0))
    return pl.pallas_call(
        _gate_matrix_kernel,
        grid=(t // tt,),
        in_specs=[spec, spec, spec],
        out_specs=pl.BlockSpec((tt, ne), lambda i: (i, 0)),
        out_shape=jax.ShapeDtypeStruct((t, ne), BF16),
        scratch_shapes=[pltpu.VMEM((tt * TOKEN_PITCH, N_KEYS), F32)],
        compiler_params=_cparams(1, 48),
        name="peer_gate_matrix",
    )(ei, ej, gate)


def _experts_kernel(alpha, sub, xb_ref, ut_ref, v_ref, gm_ref, x1_ref, g2_ref, b2_ref, o_ref, acc_ref):
    j = pl.program_id(1)

    @pl.when(j == 0)
    def _():
        acc_ref[...] = jnp.zeros_like(acc_ref)

    xb = xb_ref[...]
    ec = ut_ref.shape[1]
    total = None
    for s0 in range(0, ec, sub):
        hid = _dot(xb, ut_ref[:, s0:s0 + sub])
        act = 0.5 * hid * (1.0 + lax.erf(hid * (2.0 ** -0.5)))
        p = (gm_ref[:, s0:s0 + sub].astype(F32) * act).astype(BF16)
        part = _dot(p, v_ref[s0:s0 + sub, :])
        total = part if total is None else total + part
    acc_ref[...] += total

    @pl.when(j == pl.num_programs(1) - 1)
    def _():
        y = alpha * x1_ref[...] + acc_ref[...]
        o_ref[...] = _layer_norm(y, g2_ref[...], b2_ref[...])


def _experts(alpha, x1b, ut, v, gm, x1, g2, b2, tt, ec, sub):
    t, d = x1.shape
    ne = ut.shape[1]
    return pl.pallas_call(
        functools.partial(_experts_kernel, alpha, sub),
        grid=(t // tt, ne // ec),
        in_specs=[pl.BlockSpec((tt, d), lambda i, j: (i, 0)),
                  pl.BlockSpec((d, ec), lambda i, j: (0, j)),
                  pl.BlockSpec((ec, d), lambda i, j: (j, 0)),
                  pl.BlockSpec((tt, ec), lambda i, j: (i, j)),
                  pl.BlockSpec((tt, d), lambda i, j: (i, 0)),
                  pl.BlockSpec((1, d), lambda i, j: (0, 0)),
                  pl.BlockSpec((1, d), lambda i, j: (0, 0))],
        out_specs=pl.BlockSpec((tt, d), lambda i, j: (i, 0)),
        out_shape=jax.ShapeDtypeStruct((t, d), F32),
        scratch_shapes=[pltpu.VMEM((tt, d), F32)],
        compiler_params=_cparams(2, 48),
        name="peer_experts",
    )(x1b, ut, v, gm, x1, g2, b2)


def _layer(x, mem, rel_bias, w_in, b_gate, w_mem_kv, sinks, w_a, w_b, w_c, w_out, g1, b1,
           w_query, sub_keys, u_tab, v_tab, g2, b2, alpha):
    bsz, s, d = x.shape
    t = bsz * s
    n_a = A_GROUPS * A_HEADS

    gates_at = 3 * A_WIDTH + B_Q_WIDTH + 2 * B_KV_WIDTH + C_WIDTH
    w_perm = jnp.concatenate([w_in[:, gates_at:], w_in[:, :gates_at]], axis=1).astype(BF16)
    row_w = w_perm.shape[1]
    col_gate = 0
    col_aq = N_GATES * d
    col_ak = col_aq + A_WIDTH
    col_av = col_ak + A_WIDTH
    col_bq = col_av + A_WIDTH
    col_bk = col_bq + B_Q_WIDTH
    col_bv = col_bk + B_KV_WIDTH
    col_cq = col_bv + B_KV_WIDTH
    del col_gate

    x2 = x.reshape(t, d)
    h2 = _project(x2, w_perm, 256, row_w // 2, "in_proj")
    h3 = h2.reshape(bsz, s, row_w)

    oa, la = [], []
    for g, (win, dil) in enumerate(A_PAIRS):
        bias = _bias_tiles(rel_bias[:, g * A_HEADS:(g + 1) * A_HEADS], dil, win // dil)
        o, l = _attn_a_group(h3, bias, g, dil, col_aq, col_ak, col_av, row_w)
        oa.append(o.reshape(t, A_GROUP_WIDTH))
        la.append(l.reshape(t, A_GROUP_WIDTH))

    bias_b = _bias_tiles(rel_bias[:, n_a:n_a + B_Q_HEADS], 1, B_WINDOW - 1)
    yb = _attn_b(h3, bias_b, sinks, col_bq, col_bk, col_bv).reshape(t, B_Q_WIDTH)

    m = mem.shape[1]
    kvm = _project(mem.reshape(bsz * m, d), w_mem_kv.astype(BF16), 256, 2 * C_WIDTH, "mem_kv")
    yc = _attn_c(h3, kvm.reshape(bsz, m, 2 * C_WIDTH), col_cq, 512).reshape(t, C_WIDTH)

    x1, x1b, q = _merge(alpha, oa, la, yb, yc, h2, x2, b_gate.reshape(1, -1).astype(F32),
                        w_a.astype(BF16), w_b.astype(BF16), w_c.astype(BF16), w_out.astype(BF16),
                        g1.reshape(1, d).astype(F32), b1.reshape(1, d).astype(F32),
                        w_query.astype(BF16), 256)

    ei, ej, gate = _retrieve(q, sub_keys.astype(BF16), 256)
    gm = _gate_matrix(ei, ej, gate, 256)
    out = _experts(alpha, x1b, u_tab.astype(BF16).T, v_tab.astype(BF16), gm, x1,
                   g2.reshape(1, d).astype(F32), b2.reshape(1, d).astype(F32), 1024, 512, 256)
    return out.reshape(bsz, s, d)


def kernel(x, mem, rel_bias, w_in, b_gate, w_mem_kv, sinks, w_branch_a, w_branch_b, w_branch_c,
           w_out, ln1_g, ln1_b, peer_w_query, peer_sub_keys, peer_u, peer_v, ln2_g, ln2_b):
    depth = w_in.shape[0]
    alpha = (2.0 * depth) ** 0.25
    for l in range(depth):
        x = _layer(x, mem, rel_bias, w_in[l], b_gate[l], w_mem_kv[l], sinks[l], w_branch_a[l],
                   w_branch_b[l], w_branch_c[l], w_out[l], ln1_g[l], ln1_b[l], peer_w_query[l],
                   peer_sub_keys[l], peer_u[l], peer_v[l], ln2_g[l], ln2_b[l], alpha)
    return x
```

```python
import functools
import math

import numpy as np
import jax
import jax.numpy as jnp
from jax import lax
from jax.experimental import pallas as pl
from jax.experimental.pallas import tpu as pltpu

F32 = jnp.float32
BF16 = jnp.bfloat16

HEAD_DIM = 64
BLOCK = 128
A_PAIRS = ((128, 1), (512, 4), (2048, 16))
A_GROUPS = 3
A_HEADS = 4
A_GROUP_WIDTH = A_HEADS * HEAD_DIM
A_WIDTH = A_GROUPS * A_GROUP_WIDTH
B_Q_HEADS = 8
B_KV_HEADS = 2
B_WINDOW = 128
B_Q_WIDTH = B_Q_HEADS * HEAD_DIM
B_KV_WIDTH = B_KV_HEADS * HEAD_DIM
C_HEADS = 4
C_HEAD_DIM = 128
C_WIDTH = C_HEADS * C_HEAD_DIM
N_GATES = 3
N_BUCKETS = 32
MAX_DISTANCE = 2048
PEER_HEADS = 8
N_KEYS = 128
PEER_TOPK = 16
PEER_HALF = 64
LN_EPS = 1e-5
NEG = -1e30

V7X_VMEM_BYTES = 64 * 1024 * 1024
SUBLANES = 8
LANES = 128


def _cparams(n_grid, vmem_mb):
    return pltpu.CompilerParams(
        dimension_semantics=("arbitrary",) * n_grid,
        vmem_limit_bytes=vmem_mb * 1024 * 1024,
    )


def _dot(a, b):
    return jnp.dot(a, b, preferred_element_type=F32)


def _dot_nt(a, b):
    return lax.dot_general(a, b, (((1,), (1,)), ((), ())), preferred_element_type=F32)


def _layer_norm(y, g, b):
    mu = jnp.mean(y, axis=-1, keepdims=True)
    yc = y - mu
    var = jnp.mean(yc * yc, axis=-1, keepdims=True)
    return yc * lax.rsqrt(var + LN_EPS) * g + b


def _proj_kernel(x_ref, w_ref, o_ref):
    o_ref[...] = _dot(x_ref[...].astype(BF16), w_ref[...]).astype(o_ref.dtype)


def _project(x, w, tm, tn, name):
    m, k = x.shape
    n = w.shape[1]
    return pl.pallas_call(
        _proj_kernel,
        grid=(n // tn, m // tm),
        in_specs=[pl.BlockSpec((tm, k), lambda j, i: (i, 0)),
                  pl.BlockSpec((k, tn), lambda j, i: (0, j))],
        out_specs=pl.BlockSpec((tm, tn), lambda j, i: (i, j)),
        out_shape=jax.ShapeDtypeStruct((m, n), BF16),
        compiler_params=_cparams(2, 48),
        name=name,
    )(x, w)


def _t5_bucket(dist):
    n = np.asarray(dist, dtype=np.int32)
    max_exact = N_BUCKETS // 2
    nf = np.maximum(n, 1).astype(np.float32)
    scale = np.float32(math.log(MAX_DISTANCE / max_exact))
    large = max_exact + (np.log(nf / np.float32(max_exact)) / scale
                         * np.float32(N_BUCKETS - max_exact)).astype(np.int32)
    large = np.minimum(large, N_BUCKETS - 1)
    return np.where(n < max_exact, n, large).astype(np.int32)


def _bucket_tiles(dist_scale, max_off):
    i = np.arange(BLOCK)[:, None]
    j = np.arange(2 * BLOCK)[None, :]
    off = BLOCK + i - j
    bucket = _t5_bucket(np.clip(off, 0, max_off) * dist_scale)
    valid = (off >= 0) & (off <= max_off)
    general = np.where(valid, bucket, -1)
    first = np.where(valid & (j >= BLOCK), bucket, -1)
    return np.stack([first, general], axis=0).astype(np.int32)


def _bias_kernel(table_ref, bucket_ref, o_ref):
    h = pl.program_id(0)
    col = jnp.where(h < B_Q_HEADS, h + A_GROUPS * A_HEADS, h - B_Q_HEADS)
    for var in range(2):
        bk = bucket_ref[0, var]
        acc = jnp.full(bk.shape, NEG, F32)
        for b in range(N_BUCKETS):
            acc = jnp.where(bk == b, table_ref[b, col], acc)
        o_ref[var, 0] = acc


def _bias_tiles(rel_bias):
    kinds = [_bucket_tiles(1, B_WINDOW - 1)] + [_bucket_tiles(d, w // d) for w, d in A_PAIRS]
    buckets = jnp.asarray(np.stack(kinds, axis=0))
    n_heads = B_Q_HEADS + A_GROUPS * A_HEADS

    def kind(h):
        return jnp.where(h < B_Q_HEADS, 0, 1 + (h - B_Q_HEADS) // A_HEADS)

    return pl.pallas_call(
        _bias_kernel,
        grid=(n_heads,),
        in_specs=[pl.BlockSpec(memory_space=pltpu.SMEM),
                  pl.BlockSpec((1, 2, BLOCK, 2 * BLOCK), lambda h: (kind(h), 0, 0, 0))],
        out_specs=pl.BlockSpec((2, 1, BLOCK, 2 * BLOCK), lambda h: (0, h, 0, 0)),
        out_shape=jax.ShapeDtypeStruct((2, n_heads, BLOCK, 2 * BLOCK), F32),
        compiler_params=_cparams(1, 32),
        name="bias_tiles",
    )(rel_bias.astype(F32), buckets)


def _attn_a_kernel(q_ref, kp_ref, kc_ref, vp_ref, vc_ref, bias_ref, o_ref, l_ref):
    q = q_ref[0]
    kp = kp_ref[0]
    kc = kc_ref[0]
    vp = vp_ref[0]
    vc = vc_ref[0]
    scale = HEAD_DIM ** -0.5
    for h in range(A_HEADS):
        sl = slice(HEAD_DIM * h, HEAD_DIM * (h + 1))
        qh = q[:, sl]
        lp = _dot_nt(qh, kp[:, sl]) * scale + bias_ref[0, h, :, :BLOCK]
        lc = _dot_nt(qh, kc[:, sl]) * scale + bias_ref[0, h, :, BLOCK:]
        m = jnp.maximum(jnp.max(lp, axis=-1, keepdims=True), jnp.max(lc, axis=-1, keepdims=True))
        ep = jnp.exp(lp - m)
        ec = jnp.exp(lc - m)
        s = jnp.sum(ep, axis=-1, keepdims=True) + jnp.sum(ec, axis=-1, keepdims=True)
        inv = 1.0 / s
        o = _dot((ep * inv).astype(BF16), vp[:, sl]) + _dot((ec * inv).astype(BF16), vc[:, sl])
        o_ref[0, :, sl] = o
        l_ref[0, :, sl] = jnp.broadcast_to(m + jnp.log(s), (BLOCK, HEAD_DIM))


def _attn_a_group(h3, bias, g, r, col_q, col_k, col_v, row_w):
    bsz, s, _ = h3.shape
    ln = s // r
    nblk = ln // BLOCK
    hv = h3.reshape(bsz, ln, r * row_w)
    per_row = row_w // A_GROUP_WIDTH
    cq, ck, cv = (c // A_GROUP_WIDTH + g for c in (col_q, col_k, col_v))

    def spec(col, prev):
        if prev:
            return pl.BlockSpec((1, BLOCK, A_GROUP_WIDTH),
                                lambda b, c, n: (b, jnp.maximum(n - 1, 0), per_row * c + col))
        return pl.BlockSpec((1, BLOCK, A_GROUP_WIDTH), lambda b, c, n: (b, n, per_row * c + col))

    out_spec = pl.BlockSpec((1, BLOCK, A_GROUP_WIDTH), lambda b, c, n: (b, n, c))
    o, l = pl.pallas_call(
        _attn_a_kernel,
        grid=(bsz, r, nblk),
        in_specs=[spec(cq, False), spec(ck, True), spec(ck, False), spec(cv, True), spec(cv, False),
                  pl.BlockSpec((1, A_HEADS, BLOCK, 2 * BLOCK),
                               lambda b, c, n: (jnp.minimum(n, 1), B_Q_HEADS // A_HEADS + g, 0, 0))],
        out_specs=[out_spec, out_spec],
        out_shape=[jax.ShapeDtypeStruct((bsz, ln, r * A_GROUP_WIDTH), F32)] * 2,
        compiler_params=_cparams(3, 32),
        name=f"attn_a{g}",
    )(hv, hv, hv, hv, hv, bias)
    return o.reshape(bsz, s, A_GROUP_WIDTH), l.reshape(bsz, s, A_GROUP_WIDTH)


def _attn_b_kernel(sinks_ref, q0_ref, q1_ref, kp_ref, kc_ref, vp_ref, vc_ref, bias_ref, o_ref):
    kp = kp_ref[0]
    kc = kc_ref[0]
    vp = vp_ref[0]
    vc = vc_ref[0]
    scale = HEAD_DIM ** -0.5
    per_kv = B_Q_HEADS // B_KV_HEADS
    for h in range(B_Q_HEADS):
        kv = h // per_kv
        q = (q0_ref if kv == 0 else q1_ref)[0]
        qsl = slice(HEAD_DIM * (h % per_kv), HEAD_DIM * (h % per_kv + 1))
        ksl = slice(HEAD_DIM * kv, HEAD_DIM * (kv + 1))
        qh = q[:, qsl]
        lp = _dot_nt(qh, kp[:, ksl]) * scale + bias_ref[0, h, :, :BLOCK]
        lc = _dot_nt(qh, kc[:, ksl]) * scale + bias_ref[0, h, :, BLOCK:]
        sink = sinks_ref[h]
        m = jnp.maximum(jnp.max(lp, axis=-1, keepdims=True), jnp.max(lc, axis=-1, keepdims=True))
        m = jnp.maximum(m, sink)
        ep = jnp.exp(lp - m)
        ec = jnp.exp(lc - m)
        denom = (jnp.sum(ep, axis=-1, keepdims=True) + jnp.sum(ec, axis=-1, keepdims=True)
                 + jnp.exp(sink - m))
        inv = 1.0 / denom
        o = _dot((ep * inv).astype(BF16), vp[:, ksl]) + _dot((ec * inv).astype(BF16), vc[:, ksl])
        o_ref[0, :, HEAD_DIM * h:HEAD_DIM * (h + 1)] = o.astype(o_ref.dtype)


def _attn_b(h3, bias, sinks, col_q, col_k, col_v):
    bsz, s, _ = h3.shape
    nb = s // BLOCK
    qw = B_Q_WIDTH // B_KV_HEADS
    cq = col_q // qw
    ck = col_k // B_KV_WIDTH
    cv = col_v // B_KV_WIDTH

    def kv_spec(col, prev):
        if prev:
            return pl.BlockSpec((1, BLOCK, B_KV_WIDTH), lambda b, n: (b, jnp.maximum(n - 1, 0), col))
        return pl.BlockSpec((1, BLOCK, B_KV_WIDTH), lambda b, n: (b, n, col))

    return pl.pallas_call(
        _attn_b_kernel,
        grid=(bsz, nb),
        in_specs=[pl.BlockSpec(memory_space=pltpu.SMEM),
                  pl.BlockSpec((1, BLOCK, qw), lambda b, n: (b, n, cq)),
                  pl.BlockSpec((1, BLOCK, qw), lambda b, n: (b, n, cq + 1)),
                  kv_spec(ck, True), kv_spec(ck, False), kv_spec(cv, True), kv_spec(cv, False),
                  pl.BlockSpec((1, B_Q_HEADS, BLOCK, 2 * BLOCK),
                               lambda b, n: (jnp.minimum(n, 1), 0, 0, 0))],
        out_specs=pl.BlockSpec((1, BLOCK, B_Q_WIDTH), lambda b, n: (b, n, 0)),
        out_shape=jax.ShapeDtypeStruct((bsz, s, B_Q_WIDTH), BF16),
        compiler_params=_cparams(2, 32),
        name="attn_b",
    )(sinks.astype(F32), h3, h3, h3, h3, h3, h3, bias)


def _attn_c_kernel(q_ref, k_ref, v_ref, o_ref):
    q = q_ref[0]
    k = k_ref[0]
    v = v_ref[0]
    scale = C_HEAD_DIM ** -0.5
    for h in range(C_HEADS):
        sl = slice(C_HEAD_DIM * h, C_HEAD_DIM * (h + 1))
        logits = _dot_nt(q[:, sl], k[:, sl]) * scale
        m = jnp.max(logits, axis=-1, keepdims=True)
        e = jnp.exp(logits - m)
        inv = 1.0 / jnp.sum(e, axis=-1, keepdims=True)
        o_ref[0, :, sl] = _dot((e * inv).astype(BF16), v[:, sl]).astype(o_ref.dtype)


def _attn_c(h3, kvm, col_q, tq):
    bsz, s, _ = h3.shape
    m = kvm.shape[1]
    cq = col_q // C_WIDTH
    return pl.pallas_call(
        _attn_c_kernel,
        grid=(bsz, s // tq),
        in_specs=[pl.BlockSpec((1, tq, C_WIDTH), lambda b, n: (b, n, cq)),
                  pl.BlockSpec((1, m, C_WIDTH), lambda b, n: (b, 0, 0)),
                  pl.BlockSpec((1, m, C_WIDTH), lambda b, n: (b, 0, 1))],
        out_specs=pl.BlockSpec((1, tq, C_WIDTH), lambda b, n: (b, n, 0)),
        out_shape=jax.ShapeDtypeStruct((bsz, s, C_WIDTH), BF16),
        compiler_params=_cparams(2, 32),
        name="attn_c",
    )(h3, kvm, kvm)


def _merge_kernel(alpha, o0, o1, o2, l0, l1, l2, yb_ref, yc_ref, gate_ref, x_ref, bg_ref,
                  wa_ref, wb_ref, wc_ref, wo_ref, g1_ref, b1_ref, wq_ref,
                  x1_ref, x1b_ref, q_ref):
    d = x_ref.shape[-1]
    la, lb, lc = l0[...], l1[...], l2[...]
    m = jnp.maximum(jnp.maximum(la, lb), lc)
    wa, wb, wc = jnp.exp(la - m), jnp.exp(lb - m), jnp.exp(lc - m)
    inv = 1.0 / (wa + wb + wc)
    ya = (wa * inv) * o0[...] + (wb * inv) * o1[...] + (wc * inv) * o2[...]

    def gate(n):
        pre = gate_ref[:, n * d:(n + 1) * d].astype(F32) + bg_ref[:, n * d:(n + 1) * d]
        return jax.nn.sigmoid(pre)

    merged = (gate(0) * _dot(ya.astype(BF16), wa_ref[...])
              + gate(1) * _dot(yb_ref[...], wb_ref[...])
              + gate(2) * _dot(yc_ref[...], wc_ref[...]))
    y = alpha * x_ref[...] + _dot(merged.astype(BF16), wo_ref[...])
    x1 = _layer_norm(y, g1_ref[...], b1_ref[...])
    x1_ref[...] = x1
    x1b = x1.astype(BF16)
    x1b_ref[...] = x1b
    q_ref[...] = _dot(x1b, wq_ref[...]).astype(q_ref.dtype)


def _merge(alpha, oa, la, yb, yc, h2, x2, b_gate, w_a, w_b, w_c, w_o, g1, b1, w_q, tm):
    t, d = x2.shape
    row = lambda w: pl.BlockSpec((tm, w), lambda i: (i, 0))
    full = lambda a: pl.BlockSpec(a.shape, lambda i: (0,) * a.ndim)
    ins = [*oa, *la, yb, yc, h2, x2, b_gate, w_a, w_b, w_c, w_o, g1, b1, w_q]
    in_specs = ([row(A_GROUP_WIDTH)] * 6 + [row(B_Q_WIDTH), row(C_WIDTH), row(N_GATES * d), row(d)]
                + [full(a) for a in ins[10:]])
    return pl.pallas_call(
        functools.partial(_merge_kernel, alpha),
        grid=(t // tm,),
        in_specs=in_specs,
        out_specs=[row(d), row(d), row(w_q.shape[1])],
        out_shape=[jax.ShapeDtypeStruct((t, d), F32), jax.ShapeDtypeStruct((t, d), BF16),
                   jax.ShapeDtypeStruct((t, w_q.shape[1]), BF16)],
        compiler_params=_cparams(1, 48),
        name="merge_ln1",
    )(*ins)


def _extract_top(vals, payloads, k):
    n = vals.shape[0]
    row = lax.broadcasted_iota(jnp.int32, vals.shape, 0).astype(F32)
    tops, rows = [], []
    picked = [[] for _ in payloads]
    for _ in range(k):
        m = jnp.max(vals, axis=0, keepdims=True)
        first = jnp.min(jnp.where(vals == m, row, float(n)), axis=0, keepdims=True)
        sel = row == first
        tops.append(m)
        rows.append(first)
        for lst, p in zip(picked, payloads):
            lst.append(jnp.max(jnp.where(sel, p, -1.0), axis=0, keepdims=True))
        vals = jnp.where(sel, -jnp.inf, vals)
    return (jnp.concatenate(tops, axis=0), jnp.concatenate(rows, axis=0),
            [jnp.concatenate(lst, axis=0) for lst in picked])


def _retrieve_kernel(q_ref, keys_ref, i_ref, j_ref, g_ref):
    tt = q_ref.shape[0]
    q = q_ref[...]
    out_i, out_j, out_g = [], [], []
    for h in range(PEER_HEADS):
        v, ix = [], []
        for c in range(2):
            col = (2 * h + c) * PEER_HALF
            scores = _dot_nt(keys_ref[h, c], q[:, col:col + PEER_HALF])
            tv, ti, _ = _extract_top(scores, [], PEER_TOPK)
            v.append(tv)
            ix.append(ti)
        cv, ci, cj = [], [], []
        for k1 in range(PEER_TOPK):
            n2 = PEER_TOPK // (k1 + 1)
            cv.append(v[0][k1:k1 + 1] + v[1][:n2])
            ci.append(jnp.broadcast_to(ix[0][k1:k1 + 1], (n2, tt)))
            cj.append(ix[1][:n2])
        cv = jnp.concatenate(cv, axis=0)
        ci = jnp.concatenate(ci, axis=0)
        cj = jnp.concatenate(cj, axis=0)
        top, _, (ei, ej) = _extract_top(cv, [ci, cj], PEER_TOPK)
        e = jnp.exp(top - top[0:1])
        out_g.append(e * (1.0 / jnp.sum(e, axis=0, keepdims=True)))
        out_i.append(ei)
        out_j.append(ej)
    i_ref[...] = jnp.concatenate(out_i, axis=0).T
    j_ref[...] = jnp.concatenate(out_j, axis=0).T
    g_ref[...] = jnp.concatenate(out_g, axis=0).T


def _retrieve(q, keys, tt):
    t, w = q.shape
    npair = PEER_HEADS * PEER_TOPK
    out = pl.BlockSpec((tt, npair), lambda i: (i, 0))
    return pl.pallas_call(
        _retrieve_kernel,
        grid=(t // tt,),
        in_specs=[pl.BlockSpec((tt, w), lambda i: (i, 0)),
                  pl.BlockSpec(keys.shape, lambda i: (0, 0, 0, 0))],
        out_specs=[out, out, out],
        out_shape=[jax.ShapeDtypeStruct((t, npair), F32)] * 3,
        compiler_params=_cparams(1, 32),
        name="peer_retrieve",
    )(q, keys)


GATE_UNROLL = 8


def _gate_matrix_kernel(i_ref, j_ref, g_ref, o_ref):
    tt, npair = i_ref.shape
    key = lax.broadcasted_iota(jnp.int32, (N_KEYS, npair), 0).astype(F32).astype(BF16)
    one = jnp.ones((N_KEYS, npair), BF16)
    zero = jnp.zeros((N_KEYS, npair), BF16)

    def per_token(t, carry):
        irow = jnp.broadcast_to(i_ref[pl.ds(t, 1), :], (N_KEYS, npair)).astype(BF16)
        jrow = jnp.broadcast_to(j_ref[pl.ds(t, 1), :], (N_KEYS, npair)).astype(BF16)
        grow = jnp.broadcast_to(g_ref[pl.ds(t, 1), :], (N_KEYS, npair)).astype(BF16)
        a = jnp.where(key == irow, grow, zero)
        b = jnp.where(key == jrow, one, zero)
        o_ref[t] = _dot_nt(a, b)
        return carry

    lax.fori_loop(0, tt, per_token, 0, unroll=GATE_UNROLL)


def _gate_matrix(ei, ej, gate, tt):
    t, npair = ei.shape
    spec = pl.BlockSpec((tt, npair), lambda i: (i, 0))
    return pl.pallas_call(
        _gate_matrix_kernel,
        grid=(t // tt,),
        in_specs=[spec, spec, spec],
        out_specs=pl.BlockSpec((tt, N_KEYS, N_KEYS), lambda i: (i, 0, 0)),
        out_shape=jax.ShapeDtypeStruct((t, N_KEYS, N_KEYS), F32),
        compiler_params=_cparams(1, 48),
        name="peer_gate_matrix",
    )(ei, ej, gate)


def _experts_kernel(alpha, sub, xb_ref, ut_ref, v_ref, gm_ref, x1_ref, g2_ref, b2_ref, o_ref, acc_ref):
    j = pl.program_id(1)

    @pl.when(j == 0)
    def _():
        acc_ref[...] = jnp.zeros_like(acc_ref)

    xb = xb_ref[...]
    ec = ut_ref.shape[1]
    total = None
    for s0 in range(0, ec, sub):
        hid = _dot(xb, ut_ref[:, s0:s0 + sub])
        act = 0.5 * hid * (1.0 + lax.erf(hid * (2.0 ** -0.5)))
        gate = jnp.concatenate([gm_ref[:, i, :] for i in range(s0 // N_KEYS, (s0 + sub) // N_KEYS)],
                               axis=1)
        p = (gate * act).astype(BF16)
        part = _dot(p, v_ref[s0:s0 + sub, :])
        total = part if total is None else total + part
    acc_ref[...] += total

    @pl.when(j == pl.num_programs(1) - 1)
    def _():
        y = alpha * x1_ref[...] + acc_ref[...]
        o_ref[...] = _layer_norm(y, g2_ref[...], b2_ref[...])


def _experts(alpha, x1b, ut, v, gm, x1, g2, b2, tt, ec, sub):
    t, d = x1.shape
    ne = ut.shape[1]
    return pl.pallas_call(
        functools.partial(_experts_kernel, alpha, sub),
        grid=(t // tt, ne // ec),
        in_specs=[pl.BlockSpec((tt, d), lambda i, j: (i, 0)),
                  pl.BlockSpec((d, ec), lambda i, j: (0, j)),
                  pl.BlockSpec((ec, d), lambda i, j: (j, 0)),
                  pl.BlockSpec((tt, ec // N_KEYS, N_KEYS), lambda i, j: (i, j, 0)),
                  pl.BlockSpec((tt, d), lambda i, j: (i, 0)),
                  pl.BlockSpec((1, d), lambda i, j: (0, 0)),
                  pl.BlockSpec((1, d), lambda i, j: (0, 0))],
        out_specs=pl.BlockSpec((tt, d), lambda i, j: (i, 0)),
        out_shape=jax.ShapeDtypeStruct((t, d), F32),
        scratch_shapes=[pltpu.VMEM((tt, d), F32)],
        compiler_params=_cparams(2, 56),
        name="peer_experts",
    )(x1b, ut, v, gm, x1, g2, b2)


def _layer(x, mem, rel_bias, w_in, b_gate, w_mem_kv, sinks, w_a, w_b, w_c, w_out, g1, b1,
           w_query, sub_keys, u_tab, v_tab, g2, b2, alpha):
    bsz, s, d = x.shape
    t = bsz * s

    gates_at = 3 * A_WIDTH + B_Q_WIDTH + 2 * B_KV_WIDTH + C_WIDTH
    w_perm = jnp.concatenate([w_in[:, gates_at:], w_in[:, :gates_at]], axis=1).astype(BF16)
    row_w = w_perm.shape[1]
    col_gate = 0
    col_aq = N_GATES * d
    col_ak = col_aq + A_WIDTH
    col_av = col_ak + A_WIDTH
    col_bq = col_av + A_WIDTH
    col_bk = col_bq + B_Q_WIDTH
    col_bv = col_bk + B_KV_WIDTH
    col_cq = col_bv + B_KV_WIDTH
    del col_gate

    x2 = x.reshape(t, d)
    h2 = _project(x2, w_perm, 256, row_w // 2, "in_proj")
    h3 = h2.reshape(bsz, s, row_w)

    bias = _bias_tiles(rel_bias)
    oa, la = [], []
    for g, (win, dil) in enumerate(A_PAIRS):
        o, l = _attn_a_group(h3, bias, g, dil, col_aq, col_ak, col_av, row_w)
        oa.append(o.reshape(t, A_GROUP_WIDTH))
        la.append(l.reshape(t, A_GROUP_WIDTH))

    yb = _attn_b(h3, bias, sinks, col_bq, col_bk, col_bv).reshape(t, B_Q_WIDTH)

    m = mem.shape[1]
    kvm = _project(mem.reshape(bsz * m, d), w_mem_kv.astype(BF16), 256, 2 * C_WIDTH, "mem_kv")
    yc = _attn_c(h3, kvm.reshape(bsz, m, 2 * C_WIDTH), col_cq, 512).reshape(t, C_WIDTH)

    x1, x1b, q = _merge(alpha, oa, la, yb, yc, h2, x2, b_gate.reshape(1, -1).astype(F32),
                        w_a.astype(BF16), w_b.astype(BF16), w_c.astype(BF16), w_out.astype(BF16),
                        g1.reshape(1, d).astype(F32), b1.reshape(1, d).astype(F32),
                        w_query.astype(BF16), 256)

    ei, ej, gate = _retrieve(q, sub_keys.astype(BF16), 256)
    gm = _gate_matrix(ei, ej, gate, 256)
    out = _experts(alpha, x1b, u_tab.astype(BF16).T, v_tab.astype(BF16), gm, x1,
                   g2.reshape(1, d).astype(F32), b2.reshape(1, d).astype(F32), 1024, 1024, 256)
    return out.reshape(bsz, s, d)


def kernel(x, mem, rel_bias, w_in, b_gate, w_mem_kv, sinks, w_branch_a, w_branch_b, w_branch_c,
           w_out, ln1_g, ln1_b, peer_w_query, peer_sub_keys, peer_u, peer_v, ln2_g, ln2_b):
    depth = w_in.shape[0]
    alpha = (2.0 * depth) ** 0.25
    for l in range(depth):
        x = _layer(x, mem, rel_bias, w_in[l], b_gate[l], w_mem_kv[l], sinks[l], w_branch_a[l],
                   w_branch_b[l], w_branch_c[l], w_out[l], ln1_g[l], ln1_b[l], peer_w_query[l],
                   peer_sub_keys[l], peer_u[l], peer_v[l], ln2_g[l], ln2_b[l], alpha)
    return x
```

```python
import functools
import math

import numpy as np
import jax
import jax.numpy as jnp
from jax import lax
from jax.experimental import pallas as pl
from jax.experimental.pallas import tpu as pltpu

F32 = jnp.float32
BF16 = jnp.bfloat16

HEAD_DIM = 64
BLOCK = 128
A_PAIRS = ((128, 1), (512, 4), (2048, 16))
A_GROUPS = 3
A_HEADS = 4
A_GROUP_WIDTH = A_HEADS * HEAD_DIM
A_WIDTH = A_GROUPS * A_GROUP_WIDTH
B_Q_HEADS = 8
B_KV_HEADS = 2
B_WINDOW = 128
B_Q_WIDTH = B_Q_HEADS * HEAD_DIM
B_KV_WIDTH = B_KV_HEADS * HEAD_DIM
C_HEADS = 4
C_HEAD_DIM = 128
C_WIDTH = C_HEADS * C_HEAD_DIM
N_GATES = 3
N_BUCKETS = 32
MAX_DISTANCE = 2048
PEER_HEADS = 8
N_KEYS = 128
PEER_TOPK = 16
PEER_HALF = 64
LN_EPS = 1e-5
NEG = -1e30

V7X_VMEM_BYTES = 64 * 1024 * 1024
SUBLANES = 8
LANES = 128


def _cparams(n_grid, vmem_mb):
    return pltpu.CompilerParams(
        dimension_semantics=("arbitrary",) * n_grid,
        vmem_limit_bytes=vmem_mb * 1024 * 1024,
    )


def _dot(a, b):
    return jnp.dot(a, b, preferred_element_type=F32)


def _dot_nt(a, b):
    return lax.dot_general(a, b, (((1,), (1,)), ((), ())), preferred_element_type=F32)


def _layer_norm(y, g, b):
    mu = jnp.mean(y, axis=-1, keepdims=True)
    yc = y - mu
    var = jnp.mean(yc * yc, axis=-1, keepdims=True)
    return yc * lax.rsqrt(var + LN_EPS) * g + b


def _proj_kernel(x_ref, w_ref, o_ref):
    o_ref[...] = _dot(x_ref[...].astype(BF16), w_ref[...]).astype(o_ref.dtype)


def _project(x, w, tm, tn, name):
    m, k = x.shape
    n = w.shape[1]
    return pl.pallas_call(
        _proj_kernel,
        grid=(n // tn, m // tm),
        in_specs=[pl.BlockSpec((tm, k), lambda j, i: (i, 0)),
                  pl.BlockSpec((k, tn), lambda j, i: (0, j))],
        out_specs=pl.BlockSpec((tm, tn), lambda j, i: (i, j)),
        out_shape=jax.ShapeDtypeStruct((m, n), BF16),
        compiler_params=_cparams(2, 48),
        name=name,
    )(x, w)


def _t5_bucket(dist):
    n = np.asarray(dist, dtype=np.int32)
    max_exact = N_BUCKETS // 2
    nf = np.maximum(n, 1).astype(np.float32)
    scale = np.float32(math.log(MAX_DISTANCE / max_exact))
    large = max_exact + (np.log(nf / np.float32(max_exact)) / scale
                         * np.float32(N_BUCKETS - max_exact)).astype(np.int32)
    large = np.minimum(large, N_BUCKETS - 1)
    return np.where(n < max_exact, n, large).astype(np.int32)


def _bucket_tiles(dist_scale, max_off):
    i = np.arange(BLOCK)[:, None]
    j = np.arange(2 * BLOCK)[None, :]
    off = BLOCK + i - j
    bucket = _t5_bucket(np.clip(off, 0, max_off) * dist_scale)
    valid = (off >= 0) & (off <= max_off)
    general = np.where(valid, bucket, -1)
    first = np.where(valid & (j >= BLOCK), bucket, -1)
    return np.stack([first, general], axis=0).astype(np.int32)


def _bias_kernel(table_ref, bucket_ref, o_ref):
    h = pl.program_id(0)
    col = jnp.where(h < B_Q_HEADS, h + A_GROUPS * A_HEADS, h - B_Q_HEADS)
    for var in range(2):
        bk = bucket_ref[0, var]
        acc = jnp.full(bk.shape, NEG, F32)
        for b in range(N_BUCKETS):
            acc = jnp.where(bk == b, table_ref[b, col], acc)
        o_ref[var, 0] = acc


def _bias_tiles(rel_bias):
    kinds = [_bucket_tiles(1, B_WINDOW - 1)] + [_bucket_tiles(d, w // d) for w, d in A_PAIRS]
    buckets = jnp.asarray(np.stack(kinds, axis=0))
    n_heads = B_Q_HEADS + A_GROUPS * A_HEADS

    def kind(h):
        return jnp.where(h < B_Q_HEADS, 0, 1 + (h - B_Q_HEADS) // A_HEADS)

    return pl.pallas_call(
        _bias_kernel,
        grid=(n_heads,),
        in_specs=[pl.BlockSpec(memory_space=pltpu.SMEM),
                  pl.BlockSpec((1, 2, BLOCK, 2 * BLOCK), lambda h: (kind(h), 0, 0, 0))],
        out_specs=pl.BlockSpec((2, 1, BLOCK, 2 * BLOCK), lambda h: (0, h, 0, 0)),
        out_shape=jax.ShapeDtypeStruct((2, n_heads, BLOCK, 2 * BLOCK), F32),
        compiler_params=_cparams(1, 32),
        name="bias_tiles",
    )(rel_bias.astype(F32), buckets)


def _attn_a_kernel(q_ref, kp_ref, kc_ref, vp_ref, vc_ref, bias_ref, o_ref, l_ref):
    q = q_ref[0]
    kp = kp_ref[0]
    kc = kc_ref[0]
    vp = vp_ref[0]
    vc = vc_ref[0]
    scale = HEAD_DIM ** -0.5
    for h in range(A_HEADS):
        sl = slice(HEAD_DIM * h, HEAD_DIM * (h + 1))
        qh = q[:, sl]
        lp = _dot_nt(qh, kp[:, sl]) * scale + bias_ref[0, h, :, :BLOCK]
        lc = _dot_nt(qh, kc[:, sl]) * scale + bias_ref[0, h, :, BLOCK:]
        m = jnp.maximum(jnp.max(lp, axis=-1, keepdims=True), jnp.max(lc, axis=-1, keepdims=True))
        ep = jnp.exp(lp - m)
        ec = jnp.exp(lc - m)
        s = jnp.sum(ep, axis=-1, keepdims=True) + jnp.sum(ec, axis=-1, keepdims=True)
        inv = 1.0 / s
        o = _dot((ep * inv).astype(BF16), vp[:, sl]) + _dot((ec * inv).astype(BF16), vc[:, sl])
        o_ref[0, :, sl] = o
        l_ref[0, :, sl] = jnp.broadcast_to(m + jnp.log(s), (BLOCK, HEAD_DIM))


def _attn_a_group(h3, bias, g, r, col_q, col_k, col_v, row_w):
    bsz, s, _ = h3.shape
    ln = s // r
    nblk = ln // BLOCK
    hv = h3.reshape(bsz, ln, r * row_w)
    per_row = row_w // A_GROUP_WIDTH
    cq, ck, cv = (c // A_GROUP_WIDTH + g for c in (col_q, col_k, col_v))

    def spec(col, prev):
        if prev:
            return pl.BlockSpec((1, BLOCK, A_GROUP_WIDTH),
                                lambda b, c, n: (b, jnp.maximum(n - 1, 0), per_row * c + col))
        return pl.BlockSpec((1, BLOCK, A_GROUP_WIDTH), lambda b, c, n: (b, n, per_row * c + col))

    out_spec = pl.BlockSpec((1, BLOCK, A_GROUP_WIDTH), lambda b, c, n: (b, n, c))
    o, l = pl.pallas_call(
        _attn_a_kernel,
        grid=(bsz, r, nblk),
        in_specs=[spec(cq, False), spec(ck, True), spec(ck, False), spec(cv, True), spec(cv, False),
                  pl.BlockSpec((1, A_HEADS, BLOCK, 2 * BLOCK),
                               lambda b, c, n: (jnp.minimum(n, 1), B_Q_HEADS // A_HEADS + g, 0, 0))],
        out_specs=[out_spec, out_spec],
        out_shape=[jax.ShapeDtypeStruct((bsz, ln, r * A_GROUP_WIDTH), F32)] * 2,
        compiler_params=_cparams(3, 32),
        name=f"attn_a{g}",
    )(hv, hv, hv, hv, hv, bias)
    return o.reshape(bsz, s, A_GROUP_WIDTH), l.reshape(bsz, s, A_GROUP_WIDTH)


def _attn_b_kernel(sinks_ref, q0_ref, q1_ref, kp_ref, kc_ref, vp_ref, vc_ref, bias_ref, o_ref):
    kp = kp_ref[0]
    kc = kc_ref[0]
    vp = vp_ref[0]
    vc = vc_ref[0]
    scale = HEAD_DIM ** -0.5
    per_kv = B_Q_HEADS // B_KV_HEADS
    for h in range(B_Q_HEADS):
        kv = h // per_kv
        q = (q0_ref if kv == 0 else q1_ref)[0]
        qsl = slice(HEAD_DIM * (h % per_kv), HEAD_DIM * (h % per_kv + 1))
        ksl = slice(HEAD_DIM * kv, HEAD_DIM * (kv + 1))
        qh = q[:, qsl]
        lp = _dot_nt(qh, kp[:, ksl]) * scale + bias_ref[0, h, :, :BLOCK]
        lc = _dot_nt(qh, kc[:, ksl]) * scale + bias_ref[0, h, :, BLOCK:]
        sink = sinks_ref[h]
        m = jnp.maximum(jnp.max(lp, axis=-1, keepdims=True), jnp.max(lc, axis=-1, keepdims=True))
        m = jnp.maximum(m, sink)
        ep = jnp.exp(lp - m)
        ec = jnp.exp(lc - m)
        denom = (jnp.sum(ep, axis=-1, keepdims=True) + jnp.sum(ec, axis=-1, keepdims=True)
                 + jnp.exp(sink - m))
        inv = 1.0 / denom
        o = _dot((ep * inv).astype(BF16), vp[:, ksl]) + _dot((ec * inv).astype(BF16), vc[:, ksl])
        o_ref[0, :, HEAD_DIM * h:HEAD_DIM * (h + 1)] = o.astype(o_ref.dtype)


def _attn_b(h3, bias, sinks, col_q, col_k, col_v):
    bsz, s, _ = h3.shape
    nb = s // BLOCK
    qw = B_Q_WIDTH // B_KV_HEADS
    cq = col_q // qw
    ck = col_k // B_KV_WIDTH
    cv = col_v // B_KV_WIDTH

    def kv_spec(col, prev):
        if prev:
            return pl.BlockSpec((1, BLOCK, B_KV_WIDTH), lambda b, n: (b, jnp.maximum(n - 1, 0), col))
        return pl.BlockSpec((1, BLOCK, B_KV_WIDTH), lambda b, n: (b, n, col))

    return pl.pallas_call(
        _attn_b_kernel,
        grid=(bsz, nb),
        in_specs=[pl.BlockSpec(memory_space=pltpu.SMEM),
                  pl.BlockSpec((1, BLOCK, qw), lambda b, n: (b, n, cq)),
                  pl.BlockSpec((1, BLOCK, qw), lambda b, n: (b, n, cq + 1)),
                  kv_spec(ck, True), kv_spec(ck, False), kv_spec(cv, True), kv_spec(cv, False),
                  pl.BlockSpec((1, B_Q_HEADS, BLOCK, 2 * BLOCK),
                               lambda b, n: (jnp.minimum(n, 1), 0, 0, 0))],
        out_specs=pl.BlockSpec((1, BLOCK, B_Q_WIDTH), lambda b, n: (b, n, 0)),
        out_shape=jax.ShapeDtypeStruct((bsz, s, B_Q_WIDTH), BF16),
        compiler_params=_cparams(2, 32),
        name="attn_b",
    )(sinks.astype(F32), h3, h3, h3, h3, h3, h3, bias)


def _attn_c_kernel(q_ref, k_ref, v_ref, o_ref):
    q = q_ref[0]
    k = k_ref[0]
    v = v_ref[0]
    scale = C_HEAD_DIM ** -0.5
    for h in range(C_HEADS):
        sl = slice(C_HEAD_DIM * h, C_HEAD_DIM * (h + 1))
        logits = _dot_nt(q[:, sl], k[:, sl]) * scale
        m = jnp.max(logits, axis=-1, keepdims=True)
        e = jnp.exp(logits - m)
        inv = 1.0 / jnp.sum(e, axis=-1, keepdims=True)
        o_ref[0, :, sl] = _dot((e * inv).astype(BF16), v[:, sl]).astype(o_ref.dtype)


def _attn_c(h3, kvm, col_q, tq):
    bsz, s, _ = h3.shape
    m = kvm.shape[1]
    cq = col_q // C_WIDTH
    return pl.pallas_call(
        _attn_c_kernel,
        grid=(bsz, s // tq),
        in_specs=[pl.BlockSpec((1, tq, C_WIDTH), lambda b, n: (b, n, cq)),
                  pl.BlockSpec((1, m, C_WIDTH), lambda b, n: (b, 0, 0)),
                  pl.BlockSpec((1, m, C_WIDTH), lambda b, n: (b, 0, 1))],
        out_specs=pl.BlockSpec((1, tq, C_WIDTH), lambda b, n: (b, n, 0)),
        out_shape=jax.ShapeDtypeStruct((bsz, s, C_WIDTH), BF16),
        compiler_params=_cparams(2, 32),
        name="attn_c",
    )(h3, kvm, kvm)


def _merge_kernel(alpha, o0, o1, o2, l0, l1, l2, yb_ref, yc_ref, gate_ref, x_ref, bg_ref,
                  wa_ref, wb_ref, wc_ref, wo_ref, g1_ref, b1_ref, wq_ref,
                  x1_ref, x1b_ref, q_ref):
    d = x_ref.shape[-1]
    la, lb, lc = l0[...], l1[...], l2[...]
    m = jnp.maximum(jnp.maximum(la, lb), lc)
    wa, wb, wc = jnp.exp(la - m), jnp.exp(lb - m), jnp.exp(lc - m)
    inv = 1.0 / (wa + wb + wc)
    ya = (wa * inv) * o0[...] + (wb * inv) * o1[...] + (wc * inv) * o2[...]

    def gate(n):
        pre = gate_ref[:, n * d:(n + 1) * d].astype(F32) + bg_ref[:, n * d:(n + 1) * d]
        return jax.nn.sigmoid(pre)

    merged = (gate(0) * _dot(ya.astype(BF16), wa_ref[...])
              + gate(1) * _dot(yb_ref[...], wb_ref[...])
              + gate(2) * _dot(yc_ref[...], wc_ref[...]))
    y = alpha * x_ref[...] + _dot(merged.astype(BF16), wo_ref[...])
    x1 = _layer_norm(y, g1_ref[...], b1_ref[...])
    x1_ref[...] = x1
    x1b = x1.astype(BF16)
    x1b_ref[...] = x1b
    q_ref[...] = _dot(x1b, wq_ref[...]).astype(q_ref.dtype)


def _merge(alpha, oa, la, yb, yc, h2, x2, b_gate, w_a, w_b, w_c, w_o, g1, b1, w_q, tm):
    t, d = x2.shape
    row = lambda w: pl.BlockSpec((tm, w), lambda i: (i, 0))
    full = lambda a: pl.BlockSpec(a.shape, lambda i: (0,) * a.ndim)
    ins = [*oa, *la, yb, yc, h2, x2, b_gate, w_a, w_b, w_c, w_o, g1, b1, w_q]
    in_specs = ([row(A_GROUP_WIDTH)] * 6 + [row(B_Q_WIDTH), row(C_WIDTH), row(N_GATES * d), row(d)]
                + [full(a) for a in ins[10:]])
    return pl.pallas_call(
        functools.partial(_merge_kernel, alpha),
        grid=(t // tm,),
        in_specs=in_specs,
        out_specs=[row(d), row(d), row(w_q.shape[1])],
        out_shape=[jax.ShapeDtypeStruct((t, d), F32), jax.ShapeDtypeStruct((t, d), BF16),
                   jax.ShapeDtypeStruct((t, w_q.shape[1]), BF16)],
        compiler_params=_cparams(1, 48),
        name="merge_ln1",
    )(*ins)


def _extract_top(vals, payloads, k):
    n = vals.shape[0]
    row = lax.broadcasted_iota(jnp.int32, vals.shape, 0).astype(F32)
    tops, rows = [], []
    picked = [[] for _ in payloads]
    for _ in range(k):
        m = jnp.max(vals, axis=0, keepdims=True)
        first = jnp.min(jnp.where(vals == m, row, float(n)), axis=0, keepdims=True)
        sel = row == first
        tops.append(m)
        rows.append(first)
        for lst, p in zip(picked, payloads):
            lst.append(jnp.max(jnp.where(sel, p, -1.0), axis=0, keepdims=True))
        vals = jnp.where(sel, -jnp.inf, vals)
    return (jnp.concatenate(tops, axis=0), jnp.concatenate(rows, axis=0),
            [jnp.concatenate(lst, axis=0) for lst in picked])


def _retrieve_kernel(q_ref, keys_ref, i_ref, j_ref, g_ref):
    tt = q_ref.shape[0]
    q = q_ref[...]
    out_i, out_j, out_g = [], [], []
    for h in range(PEER_HEADS):
        v, ix = [], []
        for c in range(2):
            col = (2 * h + c) * PEER_HALF
            scores = _dot_nt(keys_ref[h, c], q[:, col:col + PEER_HALF])
            tv, ti, _ = _extract_top(scores, [], PEER_TOPK)
            v.append(tv)
            ix.append(ti)
        cv, ci, cj = [], [], []
        for k1 in range(PEER_TOPK):
            n2 = PEER_TOPK // (k1 + 1)
            cv.append(v[0][k1:k1 + 1] + v[1][:n2])
            ci.append(jnp.broadcast_to(ix[0][k1:k1 + 1], (n2, tt)))
            cj.append(ix[1][:n2])
        cv = jnp.concatenate(cv, axis=0)
        ci = jnp.concatenate(ci, axis=0)
        cj = jnp.concatenate(cj, axis=0)
        top, _, (ei, ej) = _extract_top(cv, [ci, cj], PEER_TOPK)
        e = jnp.exp(top - top[0:1])
        out_g.append(e * (1.0 / jnp.sum(e, axis=0, keepdims=True)))
        out_i.append(ei)
        out_j.append(ej)
    i_ref[...] = jnp.concatenate(out_i, axis=0).T
    j_ref[...] = jnp.concatenate(out_j, axis=0).T
    g_ref[...] = jnp.concatenate(out_g, axis=0).T


def _retrieve(q, keys, tt):
    t, w = q.shape
    npair = PEER_HEADS * PEER_TOPK
    out = pl.BlockSpec((tt, npair), lambda i: (i, 0))
    return pl.pallas_call(
        _retrieve_kernel,
        grid=(t // tt,),
        in_specs=[pl.BlockSpec((tt, w), lambda i: (i, 0)),
                  pl.BlockSpec(keys.shape, lambda i: (0, 0, 0, 0))],
        out_specs=[out, out, out],
        out_shape=[jax.ShapeDtypeStruct((t, npair), F32)] * 3,
        compiler_params=_cparams(1, 32),
        name="peer_retrieve",
    )(q, keys)


def _gate_matrix_kernel(i_ref, j_ref, g_ref, o_ref):
    tt, npair = i_ref.shape
    key = lax.broadcasted_iota(jnp.int32, (N_KEYS, npair), 0).astype(F32).astype(BF16)
    one = jnp.ones((N_KEYS, npair), BF16)
    zero = jnp.zeros((N_KEYS, npair), BF16)

    def per_group(grp, carry):
        for s in range(SUBLANES):
            t = grp * SUBLANES + s
            irow = jnp.broadcast_to(i_ref[pl.ds(t, 1), :], (N_KEYS, npair)).astype(BF16)
            jrow = jnp.broadcast_to(j_ref[pl.ds(t, 1), :], (N_KEYS, npair)).astype(BF16)
            grow = jnp.broadcast_to(g_ref[pl.ds(t, 1), :], (N_KEYS, npair)).astype(BF16)
            a = jnp.where(key == irow, grow, zero)
            b = jnp.where(key == jrow, one, zero)
            start = grp * (SUBLANES * N_KEYS) + s
            o_ref[pl.ds(start, N_KEYS, stride=SUBLANES), :] = _dot_nt(a, b)
        return carry

    lax.fori_loop(0, tt // SUBLANES, per_group, 0)


def _gate_matrix(ei, ej, gate, tt):
    t, npair = ei.shape
    spec = pl.BlockSpec((tt, npair), lambda i: (i, 0))
    gm = pl.pallas_call(
        _gate_matrix_kernel,
        grid=(t // tt,),
        in_specs=[spec, spec, spec],
        out_specs=pl.BlockSpec((tt * N_KEYS, N_KEYS), lambda i: (i, 0)),
        out_shape=jax.ShapeDtypeStruct((t * N_KEYS, N_KEYS), F32),
        compiler_params=_cparams(1, 48),
        name="peer_gate_matrix",
    )(ei, ej, gate)
    return gm.reshape(t // SUBLANES, N_KEYS, SUBLANES, N_KEYS)


def _experts_kernel(alpha, sub, xb_ref, ut_ref, v_ref, gm_ref, x1_ref, g2_ref, b2_ref, o_ref, acc_ref):
    j = pl.program_id(1)

    @pl.when(j == 0)
    def _():
        acc_ref[...] = jnp.zeros_like(acc_ref)

    tt = xb_ref.shape[0]
    n_i = gm_ref.shape[1]
    for r0 in range(0, tt, sub):
        hid = _dot(xb_ref[r0:r0 + sub, :], ut_ref[...])
        act = 0.5 * hid * (1.0 + lax.erf(hid * (2.0 ** -0.5)))
        grp = slice(r0 // SUBLANES, (r0 + sub) // SUBLANES)
        gate = jnp.concatenate([gm_ref[grp, i].reshape(sub, N_KEYS) for i in range(n_i)], axis=1)
        p = (gate * act).astype(BF16)
        acc_ref[r0:r0 + sub, :] += _dot(p, v_ref[...])

    @pl.when(j == pl.num_programs(1) - 1)
    def _():
        y = alpha * x1_ref[...] + acc_ref[...]
        o_ref[...] = _layer_norm(y, g2_ref[...], b2_ref[...])


def _experts(alpha, x1b, ut, v, gm, x1, g2, b2, tt, ec, sub):
    t, d = x1.shape
    ne = ut.shape[1]
    return pl.pallas_call(
        functools.partial(_experts_kernel, alpha, sub),
        grid=(t // tt, ne // ec),
        in_specs=[pl.BlockSpec((tt, d), lambda i, j: (i, 0)),
                  pl.BlockSpec((d, ec), lambda i, j: (0, j)),
                  pl.BlockSpec((ec, d), lambda i, j: (j, 0)),
                  pl.BlockSpec((tt // SUBLANES, ec // N_KEYS, SUBLANES, N_KEYS),
                               lambda i, j: (i, j, 0, 0)),
                  pl.BlockSpec((tt, d), lambda i, j: (i, 0)),
                  pl.BlockSpec((1, d), lambda i, j: (0, 0)),
                  pl.BlockSpec((1, d), lambda i, j: (0, 0))],
        out_specs=pl.BlockSpec((tt, d), lambda i, j: (i, 0)),
        out_shape=jax.ShapeDtypeStruct((t, d), F32),
        scratch_shapes=[pltpu.VMEM((tt, d), F32)],
        compiler_params=_cparams(2, 56),
        name="peer_experts",
    )(x1b, ut, v, gm, x1, g2, b2)


def _layer(x, mem, rel_bias, w_in, b_gate, w_mem_kv, sinks, w_a, w_b, w_c, w_out, g1, b1,
           w_query, sub_keys, u_tab, v_tab, g2, b2, alpha):
    bsz, s, d = x.shape
    t = bsz * s

    gates_at = 3 * A_WIDTH + B_Q_WIDTH + 2 * B_KV_WIDTH + C_WIDTH
    w_perm = jnp.concatenate([w_in[:, gates_at:], w_in[:, :gates_at]], axis=1).astype(BF16)
    row_w = w_perm.shape[1]
    col_gate = 0
    col_aq = N_GATES * d
    col_ak = col_aq + A_WIDTH
    col_av = col_ak + A_WIDTH
    col_bq = col_av + A_WIDTH
    col_bk = col_bq + B_Q_WIDTH
    col_bv = col_bk + B_KV_WIDTH
    col_cq = col_bv + B_KV_WIDTH
    del col_gate

    x2 = x.reshape(t, d)
    h2 = _project(x2, w_perm, 256, row_w // 2, "in_proj")
    h3 = h2.reshape(bsz, s, row_w)

    bias = _bias_tiles(rel_bias)
    oa, la = [], []
    for g, (win, dil) in enumerate(A_PAIRS):
        o, l = _attn_a_group(h3, bias, g, dil, col_aq, col_ak, col_av, row_w)
        oa.append(o.reshape(t, A_GROUP_WIDTH))
        la.append(l.reshape(t, A_GROUP_WIDTH))

    yb = _attn_b(h3, bias, sinks, col_bq, col_bk, col_bv).reshape(t, B_Q_WIDTH)

    m = mem.shape[1]
    kvm = _project(mem.reshape(bsz * m, d), w_mem_kv.astype(BF16), 256, 2 * C_WIDTH, "mem_kv")
    yc = _attn_c(h3, kvm.reshape(bsz, m, 2 * C_WIDTH), col_cq, 512).reshape(t, C_WIDTH)

    x1, x1b, q = _merge(alpha, oa, la, yb, yc, h2, x2, b_gate.reshape(1, -1).astype(F32),
                        w_a.astype(BF16), w_b.astype(BF16), w_c.astype(BF16), w_out.astype(BF16),
                        g1.reshape(1, d).astype(F32), b1.reshape(1, d).astype(F32),
                        w_query.astype(BF16), 256)

    ei, ej, gate = _retrieve(q, sub_keys.astype(BF16), 256)
    gm = _gate_matrix(ei, ej, gate, 256)
    out = _experts(alpha, x1b, u_tab.astype(BF16).T, v_tab.astype(BF16), gm, x1,
                   g2.reshape(1, d).astype(F32), b2.reshape(1, d).astype(F32), 1024, 1024, 256)
    return out.reshape(bsz, s, d)


def kernel(x, mem, rel_bias, w_in, b_gate, w_mem_kv, sinks, w_branch_a, w_branch_b, w_branch_c,
           w_out, ln1_g, ln1_b, peer_w_query, peer_sub_keys, peer_u, peer_v, ln2_g, ln2_b):
    depth = w_in.shape[0]
    alpha = (2.0 * depth) ** 0.25
    for l in range(depth):
        x = _layer(x, mem, rel_bias, w_in[l], b_gate[l], w_mem_kv[l], sinks[l], w_branch_a[l],
                   w_branch_b[l], w_branch_c[l], w_out[l], ln1_g[l], ln1_b[l], peer_w_query[l],
                   peer_sub_keys[l], peer_u[l], peer_v[l], ln2_g[l], ln2_b[l], alpha)
    return x
```

```python
import functools
import math

import numpy as np
import jax
import jax.numpy as jnp
from jax import lax
from jax.experimental import pallas as pl
from jax.experimental.pallas import tpu as pltpu

F32 = jnp.float32
BF16 = jnp.bfloat16

HEAD_DIM = 64
BLOCK = 128
A_PAIRS = ((128, 1), (512, 4), (2048, 16))
A_GROUPS = 3
A_HEADS = 4
A_GROUP_WIDTH = A_HEADS * HEAD_DIM
A_WIDTH = A_GROUPS * A_GROUP_WIDTH
B_Q_HEADS = 8
B_KV_HEADS = 2
B_WINDOW = 128
B_Q_WIDTH = B_Q_HEADS * HEAD_DIM
B_KV_WIDTH = B_KV_HEADS * HEAD_DIM
C_HEADS = 4
C_HEAD_DIM = 128
C_WIDTH = C_HEADS * C_HEAD_DIM
N_GATES = 3
N_BUCKETS = 32
MAX_DISTANCE = 2048
PEER_HEADS = 8
N_KEYS = 128
PEER_TOPK = 16
PEER_HALF = 64
LN_EPS = 1e-5
NEG = -1e30

V7X_VMEM_BYTES = 64 * 1024 * 1024
SUBLANES = 8
LANES = 128


def _cparams(n_grid, vmem_mb):
    return pltpu.CompilerParams(
        dimension_semantics=("arbitrary",) * n_grid,
        vmem_limit_bytes=vmem_mb * 1024 * 1024,
    )


def _dot(a, b):
    return jnp.dot(a, b, preferred_element_type=F32)


def _dot_nt(a, b):
    return lax.dot_general(a, b, (((1,), (1,)), ((), ())), preferred_element_type=F32)


def _layer_norm(y, g, b):
    mu = jnp.mean(y, axis=-1, keepdims=True)
    yc = y - mu
    var = jnp.mean(yc * yc, axis=-1, keepdims=True)
    return yc * lax.rsqrt(var + LN_EPS) * g + b


def _proj_kernel(x_ref, w_ref, o_ref):
    o_ref[...] = _dot(x_ref[...].astype(BF16), w_ref[...]).astype(o_ref.dtype)


def _project(x, w, tm, tn, name):
    m, k = x.shape
    n = w.shape[1]
    return pl.pallas_call(
        _proj_kernel,
        grid=(n // tn, m // tm),
        in_specs=[pl.BlockSpec((tm, k), lambda j, i: (i, 0)),
                  pl.BlockSpec((k, tn), lambda j, i: (0, j))],
        out_specs=pl.BlockSpec((tm, tn), lambda j, i: (i, j)),
        out_shape=jax.ShapeDtypeStruct((m, n), BF16),
        compiler_params=_cparams(2, 48),
        name=name,
    )(x, w)


def _in_proj_kernel(x_ref, wn_ref, w1_ref, w2_ref, p1_ref, p2_ref, hn_ref, h1_ref, h2_ref):
    tm = x_ref.shape[0]
    xb = x_ref[...].astype(BF16)
    hn_ref[...] = _dot(xb, wn_ref[...]).astype(hn_ref.dtype)
    for w_ref, p_ref, h_ref in ((w1_ref, p1_ref, h1_ref), (w2_ref, p2_ref, h2_ref)):
        r = h_ref.shape[1]
        h = _dot(xb, w_ref[...]).astype(BF16)
        hp = _dot(p_ref[...], h).astype(h_ref.dtype)
        h_ref[0] = hp.reshape(r, tm // r, hp.shape[-1])


def _class_major_permutation(tm, r):
    dst = np.arange(tm)
    src = (dst % (tm // r)) * r + dst // (tm // r)
    p = np.zeros((tm, tm), np.float32)
    p[dst, src] = 1.0
    return jnp.asarray(p, BF16)


def _in_proj(x2, bsz, w_nat, w_dil, dils, tm):
    t, d = x2.shape
    s = t // bsz
    nt = s // tm
    full = lambda a: pl.BlockSpec(a.shape, lambda b, i: (0, 0))
    dil_spec = lambda r, w: pl.BlockSpec((1, r, tm // r, w), lambda b, i: (b, 0, i, 0))
    perms = [_class_major_permutation(tm, r) for r in dils]
    return pl.pallas_call(
        _in_proj_kernel,
        grid=(bsz, nt),
        in_specs=[pl.BlockSpec((tm, d), lambda b, i: (b * nt + i, 0)),
                  full(w_nat), full(w_dil[0]), full(w_dil[1]), full(perms[0]), full(perms[1])],
        out_specs=[pl.BlockSpec((tm, w_nat.shape[1]), lambda b, i: (b * nt + i, 0)),
                   dil_spec(dils[0], w_dil[0].shape[1]), dil_spec(dils[1], w_dil[1].shape[1])],
        out_shape=[jax.ShapeDtypeStruct((t, w_nat.shape[1]), BF16),
                   jax.ShapeDtypeStruct((bsz, dils[0], s // dils[0], w_dil[0].shape[1]), BF16),
                   jax.ShapeDtypeStruct((bsz, dils[1], s // dils[1], w_dil[1].shape[1]), BF16)],
        compiler_params=_cparams(2, 56),
        name="in_proj",
    )(x2, w_nat, *w_dil, *perms)


def _t5_bucket(dist):
    n = np.asarray(dist, dtype=np.int32)
    max_exact = N_BUCKETS // 2
    nf = np.maximum(n, 1).astype(np.float32)
    scale = np.float32(math.log(MAX_DISTANCE / max_exact))
    large = max_exact + (np.log(nf / np.float32(max_exact)) / scale
                         * np.float32(N_BUCKETS - max_exact)).astype(np.int32)
    large = np.minimum(large, N_BUCKETS - 1)
    return np.where(n < max_exact, n, large).astype(np.int32)


def _bucket_tiles(dist_scale, max_off):
    i = np.arange(BLOCK)[:, None]
    j = np.arange(2 * BLOCK)[None, :]
    off = BLOCK + i - j
    bucket = _t5_bucket(np.clip(off, 0, max_off) * dist_scale)
    valid = (off >= 0) & (off <= max_off)
    general = np.where(valid, bucket, -1)
    first = np.where(valid & (j >= BLOCK), bucket, -1)
    return np.stack([first, general], axis=0).astype(np.int32)


def _bias_kernel(table_ref, bucket_ref, o_ref):
    h = pl.program_id(0)
    col = jnp.where(h < B_Q_HEADS, h + A_GROUPS * A_HEADS, h - B_Q_HEADS)
    for var in range(2):
        bk = bucket_ref[0, var]
        acc = jnp.full(bk.shape, NEG, F32)
        for b in range(N_BUCKETS):
            acc = jnp.where(bk == b, table_ref[b, col], acc)
        o_ref[var, 0] = acc


def _bias_tiles(rel_bias):
    kinds = [_bucket_tiles(1, B_WINDOW - 1)] + [_bucket_tiles(d, w // d) for w, d in A_PAIRS]
    buckets = jnp.asarray(np.stack(kinds, axis=0))
    n_heads = B_Q_HEADS + A_GROUPS * A_HEADS

    def kind(h):
        return jnp.where(h < B_Q_HEADS, 0, 1 + (h - B_Q_HEADS) // A_HEADS)

    return pl.pallas_call(
        _bias_kernel,
        grid=(n_heads,),
        in_specs=[pl.BlockSpec(memory_space=pltpu.SMEM),
                  pl.BlockSpec((1, 2, BLOCK, 2 * BLOCK), lambda h: (kind(h), 0, 0, 0))],
        out_specs=pl.BlockSpec((2, 1, BLOCK, 2 * BLOCK), lambda h: (0, h, 0, 0)),
        out_shape=jax.ShapeDtypeStruct((2, n_heads, BLOCK, 2 * BLOCK), F32),
        compiler_params=_cparams(1, 32),
        name="bias_tiles",
    )(rel_bias.astype(F32), buckets)


def _attn_a_kernel(r, q_ref, kp_ref, kc_ref, vp_ref, vc_ref, bias_ref, o0_ref, o1_ref, l0_ref, l1_ref):
    c = pl.program_id(2)
    q = q_ref[0, 0]
    kp = kp_ref[0, 0]
    kc = kc_ref[0, 0]
    vp = vp_ref[0, 0]
    vc = vc_ref[0, 0]
    scale = HEAD_DIM ** -0.5
    rows = pl.ds(c, BLOCK, stride=r) if r > 1 else slice(None)
    outs, lses = [], []
    for h in range(A_HEADS):
        sl = slice(HEAD_DIM * h, HEAD_DIM * (h + 1))
        qh = q[:, sl]
        lp = _dot_nt(qh, kp[:, sl]) * scale + bias_ref[0, h, :, :BLOCK]
        lc = _dot_nt(qh, kc[:, sl]) * scale + bias_ref[0, h, :, BLOCK:]
        m = jnp.maximum(jnp.max(lp, axis=-1, keepdims=True), jnp.max(lc, axis=-1, keepdims=True))
        ep = jnp.exp(lp - m)
        ec = jnp.exp(lc - m)
        s = jnp.sum(ep, axis=-1, keepdims=True) + jnp.sum(ec, axis=-1, keepdims=True)
        inv = 1.0 / s
        outs.append(_dot((ep * inv).astype(BF16), vp[:, sl]) + _dot((ec * inv).astype(BF16), vc[:, sl]))
        lses.append(jnp.broadcast_to(m + jnp.log(s), (BLOCK, HEAD_DIM)))
    o0_ref[rows, :] = jnp.concatenate(outs[:2], axis=1)
    o1_ref[rows, :] = jnp.concatenate(outs[2:], axis=1)
    l0_ref[rows, :] = jnp.concatenate(lses[:2], axis=1)
    l1_ref[rows, :] = jnp.concatenate(lses[2:], axis=1)


def _attn_a_group(hr, bias, g, col_q, col_k, col_v):
    bsz, r, ln, _ = hr.shape
    nblk = ln // BLOCK
    half = A_GROUP_WIDTH // 2

    def spec(col, prev):
        if prev:
            return pl.BlockSpec((1, 1, BLOCK, A_GROUP_WIDTH),
                                lambda b, n, c: (b, c, jnp.maximum(n - 1, 0), col))
        return pl.BlockSpec((1, 1, BLOCK, A_GROUP_WIDTH), lambda b, n, c: (b, c, n, col))

    out_spec = pl.BlockSpec((BLOCK * r, half), lambda b, n, c: (b * nblk + n, 0))
    return pl.pallas_call(
        functools.partial(_attn_a_kernel, r),
        grid=(bsz, nblk, r),
        in_specs=[spec(col_q, False), spec(col_k, True), spec(col_k, False),
                  spec(col_v, True), spec(col_v, False),
                  pl.BlockSpec((1, A_HEADS, BLOCK, 2 * BLOCK),
                               lambda b, n, c: (jnp.minimum(n, 1), B_Q_HEADS // A_HEADS + g, 0, 0))],
        out_specs=[out_spec] * 4,
        out_shape=[jax.ShapeDtypeStruct((bsz * ln * r, half), F32)] * 4,
        compiler_params=_cparams(3, 32),
        name=f"attn_a{g}",
    )(hr, hr, hr, hr, hr, bias)


def _attn_b_kernel(sinks_ref, q0_ref, q1_ref, kp_ref, kc_ref, vp_ref, vc_ref, bias_ref, o_ref):
    kp = kp_ref[0]
    kc = kc_ref[0]
    vp = vp_ref[0]
    vc = vc_ref[0]
    scale = HEAD_DIM ** -0.5
    per_kv = B_Q_HEADS // B_KV_HEADS
    for h in range(B_Q_HEADS):
        kv = h // per_kv
        q = (q0_ref if kv == 0 else q1_ref)[0]
        qsl = slice(HEAD_DIM * (h % per_kv), HEAD_DIM * (h % per_kv + 1))
        ksl = slice(HEAD_DIM * kv, HEAD_DIM * (kv + 1))
        qh = q[:, qsl]
        lp = _dot_nt(qh, kp[:, ksl]) * scale + bias_ref[0, h, :, :BLOCK]
        lc = _dot_nt(qh, kc[:, ksl]) * scale + bias_ref[0, h, :, BLOCK:]
        sink = sinks_ref[h]
        m = jnp.maximum(jnp.max(lp, axis=-1, keepdims=True), jnp.max(lc, axis=-1, keepdims=True))
        m = jnp.maximum(m, sink)
        ep = jnp.exp(lp - m)
        ec = jnp.exp(lc - m)
        denom = (jnp.sum(ep, axis=-1, keepdims=True) + jnp.sum(ec, axis=-1, keepdims=True)
                 + jnp.exp(sink - m))
        inv = 1.0 / denom
        o = _dot((ep * inv).astype(BF16), vp[:, ksl]) + _dot((ec * inv).astype(BF16), vc[:, ksl])
        o_ref[0, :, HEAD_DIM * h:HEAD_DIM * (h + 1)] = o.astype(o_ref.dtype)


def _attn_b(h3, bias, sinks, col_q, col_k, col_v):
    bsz, s, _ = h3.shape
    nb = s // BLOCK
    qw = B_Q_WIDTH // B_KV_HEADS
    cq = col_q // qw
    ck = col_k // B_KV_WIDTH
    cv = col_v // B_KV_WIDTH

    def kv_spec(col, prev):
        if prev:
            return pl.BlockSpec((1, BLOCK, B_KV_WIDTH), lambda b, n: (b, jnp.maximum(n - 1, 0), col))
        return pl.BlockSpec((1, BLOCK, B_KV_WIDTH), lambda b, n: (b, n, col))

    return pl.pallas_call(
        _attn_b_kernel,
        grid=(bsz, nb),
        in_specs=[pl.BlockSpec(memory_space=pltpu.SMEM),
                  pl.BlockSpec((1, BLOCK, qw), lambda b, n: (b, n, cq)),
                  pl.BlockSpec((1, BLOCK, qw), lambda b, n: (b, n, cq + 1)),
                  kv_spec(ck, True), kv_spec(ck, False), kv_spec(cv, True), kv_spec(cv, False),
                  pl.BlockSpec((1, B_Q_HEADS, BLOCK, 2 * BLOCK),
                               lambda b, n: (jnp.minimum(n, 1), 0, 0, 0))],
        out_specs=pl.BlockSpec((1, BLOCK, B_Q_WIDTH), lambda b, n: (b, n, 0)),
        out_shape=jax.ShapeDtypeStruct((bsz, s, B_Q_WIDTH), BF16),
        compiler_params=_cparams(2, 32),
        name="attn_b",
    )(sinks.astype(F32), h3, h3, h3, h3, h3, h3, bias)


def _attn_c_kernel(q_ref, k_ref, v_ref, o_ref):
    q = q_ref[0]
    k = k_ref[0]
    v = v_ref[0]
    scale = C_HEAD_DIM ** -0.5
    for h in range(C_HEADS):
        sl = slice(C_HEAD_DIM * h, C_HEAD_DIM * (h + 1))
        logits = _dot_nt(q[:, sl], k[:, sl]) * scale
        m = jnp.max(logits, axis=-1, keepdims=True)
        e = jnp.exp(logits - m)
        inv = 1.0 / jnp.sum(e, axis=-1, keepdims=True)
        o_ref[0, :, sl] = _dot((e * inv).astype(BF16), v[:, sl]).astype(o_ref.dtype)


def _attn_c(h3, kvm, col_q, tq):
    bsz, s, _ = h3.shape
    m = kvm.shape[1]
    cq = col_q // C_WIDTH
    return pl.pallas_call(
        _attn_c_kernel,
        grid=(bsz, s // tq),
        in_specs=[pl.BlockSpec((1, tq, C_WIDTH), lambda b, n: (b, n, cq)),
                  pl.BlockSpec((1, m, C_WIDTH), lambda b, n: (b, 0, 0)),
                  pl.BlockSpec((1, m, C_WIDTH), lambda b, n: (b, 0, 1))],
        out_specs=pl.BlockSpec((1, tq, C_WIDTH), lambda b, n: (b, n, 0)),
        out_shape=jax.ShapeDtypeStruct((bsz, s, C_WIDTH), BF16),
        compiler_params=_cparams(2, 32),
        name="attn_c",
    )(h3, kvm, kvm)


def _merge_kernel(alpha, *refs):
    group_refs, refs = refs[:4 * A_GROUPS], refs[4 * A_GROUPS:]
    (yb_ref, yc_ref, gate_ref, x_ref, bg_ref, wa_ref, wb_ref, wc_ref, wo_ref, g1_ref, b1_ref,
     wq_ref, x1_ref, x1b_ref, q_ref) = refs
    d = x_ref.shape[-1]
    halves = []
    for hf in range(2):
        o0, o1, o2 = (group_refs[4 * g + hf][...] for g in range(A_GROUPS))
        la, lb, lc = (group_refs[4 * g + 2 + hf][...] for g in range(A_GROUPS))
        m = jnp.maximum(jnp.maximum(la, lb), lc)
        wa, wb, wc = jnp.exp(la - m), jnp.exp(lb - m), jnp.exp(lc - m)
        inv = 1.0 / (wa + wb + wc)
        halves.append((wa * inv) * o0 + (wb * inv) * o1 + (wc * inv) * o2)
    ya = jnp.concatenate(halves, axis=1)

    def gate(n):
        pre = gate_ref[:, n * d:(n + 1) * d].astype(F32) + bg_ref[:, n * d:(n + 1) * d]
        return jax.nn.sigmoid(pre)

    merged = (gate(0) * _dot(ya.astype(BF16), wa_ref[...])
              + gate(1) * _dot(yb_ref[...], wb_ref[...])
              + gate(2) * _dot(yc_ref[...], wc_ref[...]))
    y = alpha * x_ref[...] + _dot(merged.astype(BF16), wo_ref[...])
    x1 = _layer_norm(y, g1_ref[...], b1_ref[...])
    x1_ref[...] = x1
    x1b = x1.astype(BF16)
    x1b_ref[...] = x1b
    q_ref[...] = _dot(x1b, wq_ref[...]).astype(q_ref.dtype)


def _merge(alpha, groups, yb, yc, h2, x2, b_gate, w_a, w_b, w_c, w_o, g1, b1, w_q, tm):
    t, d = x2.shape
    row = lambda w: pl.BlockSpec((tm, w), lambda i: (i, 0))
    full = lambda a: pl.BlockSpec(a.shape, lambda i: (0,) * a.ndim)
    weights = [b_gate, w_a, w_b, w_c, w_o, g1, b1, w_q]
    ins = [*groups, yb, yc, h2, x2, *weights]
    in_specs = ([row(A_GROUP_WIDTH // 2)] * len(groups)
                + [row(B_Q_WIDTH), row(C_WIDTH), row(N_GATES * d), row(d)]
                + [full(a) for a in weights])
    return pl.pallas_call(
        functools.partial(_merge_kernel, alpha),
        grid=(t // tm,),
        in_specs=in_specs,
        out_specs=[row(d), row(d), row(w_q.shape[1])],
        out_shape=[jax.ShapeDtypeStruct((t, d), F32), jax.ShapeDtypeStruct((t, d), BF16),
                   jax.ShapeDtypeStruct((t, w_q.shape[1]), BF16)],
        compiler_params=_cparams(1, 48),
        name="merge_ln1",
    )(*ins)


def _extract_top(vals, payloads, k):
    n = vals.shape[0]
    row = lax.broadcasted_iota(jnp.int32, vals.shape, 0).astype(F32)
    tops, rows = [], []
    picked = [[] for _ in payloads]
    for _ in range(k):
        m = jnp.max(vals, axis=0, keepdims=True)
        first = jnp.min(jnp.where(vals == m, row, float(n)), axis=0, keepdims=True)
        sel = row == first
        tops.append(m)
        rows.append(first)
        for lst, p in zip(picked, payloads):
            lst.append(jnp.max(jnp.where(sel, p, -1.0), axis=0, keepdims=True))
        vals = jnp.where(sel, -jnp.inf, vals)
    return (jnp.concatenate(tops, axis=0), jnp.concatenate(rows, axis=0),
            [jnp.concatenate(lst, axis=0) for lst in picked])


def _retrieve_kernel(q_ref, keys_ref, i_ref, j_ref, g_ref):
    tt = q_ref.shape[0]
    q = q_ref[...]
    out_i, out_j, out_g = [], [], []
    for h in range(PEER_HEADS):
        v, ix = [], []
        for c in range(2):
            col = (2 * h + c) * PEER_HALF
            scores = _dot_nt(keys_ref[h, c], q[:, col:col + PEER_HALF])
            tv, ti, _ = _extract_top(scores, [], PEER_TOPK)
            v.append(tv)
            ix.append(ti)
        cv, ci, cj = [], [], []
        for k1 in range(PEER_TOPK):
            n2 = PEER_TOPK // (k1 + 1)
            cv.append(v[0][k1:k1 + 1] + v[1][:n2])
            ci.append(jnp.broadcast_to(ix[0][k1:k1 + 1], (n2, tt)))
            cj.append(ix[1][:n2])
        cv = jnp.concatenate(cv, axis=0)
        ci = jnp.concatenate(ci, axis=0)
        cj = jnp.concatenate(cj, axis=0)
        top, _, (ei, ej) = _extract_top(cv, [ci, cj], PEER_TOPK)
        e = jnp.exp(top - top[0:1])
        out_g.append(e * (1.0 / jnp.sum(e, axis=0, keepdims=True)))
        out_i.append(ei)
        out_j.append(ej)
    i_ref[...] = jnp.concatenate(out_i, axis=0).T
    j_ref[...] = jnp.concatenate(out_j, axis=0).T
    g_ref[...] = jnp.concatenate(out_g, axis=0).T


def _retrieve(q, keys, tt):
    t, w = q.shape
    npair = PEER_HEADS * PEER_TOPK
    out = pl.BlockSpec((tt, npair), lambda i: (i, 0))
    return pl.pallas_call(
        _retrieve_kernel,
        grid=(t // tt,),
        in_specs=[pl.BlockSpec((tt, w), lambda i: (i, 0)),
                  pl.BlockSpec(keys.shape, lambda i: (0, 0, 0, 0))],
        out_specs=[out, out, out],
        out_shape=[jax.ShapeDtypeStruct((t, npair), F32)] * 3,
        compiler_params=_cparams(1, 32),
        name="peer_retrieve",
    )(q, keys)


GATE_TOKENS_PER_TRIP = 64


def _gate_matrix_kernel(i_ref, j_ref, g_ref, o_ref):
    tt, npair = i_ref.shape
    key = lax.broadcasted_iota(jnp.int32, (N_KEYS, npair), 0).astype(F32).astype(BF16)
    one = jnp.ones((N_KEYS, npair), BF16)
    zero = jnp.zeros((N_KEYS, npair), BF16)

    def per_trip(trip, carry):
        for u in range(GATE_TOKENS_PER_TRIP):
            t = trip * GATE_TOKENS_PER_TRIP + u
            irow = jnp.broadcast_to(i_ref[pl.ds(t, 1), :], (N_KEYS, npair)).astype(BF16)
            jrow = jnp.broadcast_to(j_ref[pl.ds(t, 1), :], (N_KEYS, npair)).astype(BF16)
            grow = jnp.broadcast_to(g_ref[pl.ds(t, 1), :], (N_KEYS, npair)).astype(BF16)
            a = jnp.where(key == irow, grow, zero)
            b = jnp.where(key == jrow, one, zero)
            grp, s = divmod(u, SUBLANES)
            start = (trip * (GATE_TOKENS_PER_TRIP // SUBLANES) + grp) * (SUBLANES * N_KEYS) + s
            o_ref[pl.ds(start, N_KEYS, stride=SUBLANES), :] = _dot_nt(a, b)
        return carry

    lax.fori_loop(0, tt // GATE_TOKENS_PER_TRIP, per_trip, 0)


def _gate_matrix(ei, ej, gate, tt):
    t, npair = ei.shape
    spec = pl.BlockSpec((tt, npair), lambda i: (i, 0))
    gm = pl.pallas_call(
        _gate_matrix_kernel,
        grid=(t // tt,),
        in_specs=[spec, spec, spec],
        out_specs=pl.BlockSpec((tt * N_KEYS, N_KEYS), lambda i: (i, 0)),
        out_shape=jax.ShapeDtypeStruct((t * N_KEYS, N_KEYS), F32),
        compiler_params=_cparams(1, 48),
        name="peer_gate_matrix",
    )(ei, ej, gate)
    return gm.reshape(t // SUBLANES, N_KEYS, SUBLANES, N_KEYS)


def _experts_kernel(alpha, sub, xb_ref, ut_ref, v_ref, gm_ref, x1_ref, g2_ref, b2_ref, o_ref, acc_ref):
    j = pl.program_id(1)

    @pl.when(j == 0)
    def _():
        acc_ref[...] = jnp.zeros_like(acc_ref)

    tt = xb_ref.shape[0]
    n_i = gm_ref.shape[1]
    for r0 in range(0, tt, sub):
        hid = _dot(xb_ref[r0:r0 + sub, :], ut_ref[...])
        act = 0.5 * hid * (1.0 + lax.erf(hid * (2.0 ** -0.5)))
        grp = slice(r0 // SUBLANES, (r0 + sub) // SUBLANES)
        gate = jnp.concatenate([gm_ref[grp, i].reshape(sub, N_KEYS) for i in range(n_i)], axis=1)
        p = (gate * act).astype(BF16)
        acc_ref[r0:r0 + sub, :] += _dot(p, v_ref[...])

    @pl.when(j == pl.num_programs(1) - 1)
    def _():
        y = alpha * x1_ref[...] + acc_ref[...]
        o_ref[...] = _layer_norm(y, g2_ref[...], b2_ref[...])


def _experts(alpha, x1b, ut, v, gm, x1, g2, b2, tt, ec, sub):
    t, d = x1.shape
    ne = ut.shape[1]
    return pl.pallas_call(
        functools.partial(_experts_kernel, alpha, sub),
        grid=(t // tt, ne // ec),
        in_specs=[pl.BlockSpec((tt, d), lambda i, j: (i, 0)),
                  pl.BlockSpec((d, ec), lambda i, j: (0, j)),
                  pl.BlockSpec((ec, d), lambda i, j: (j, 0)),
                  pl.BlockSpec((tt // SUBLANES, ec // N_KEYS, SUBLANES, N_KEYS),
                               lambda i, j: (i, j, 0, 0)),
                  pl.BlockSpec((tt, d), lambda i, j: (i, 0)),
                  pl.BlockSpec((1, d), lambda i, j: (0, 0)),
                  pl.BlockSpec((1, d), lambda i, j: (0, 0))],
        out_specs=pl.BlockSpec((tt, d), lambda i, j: (i, 0)),
        out_shape=jax.ShapeDtypeStruct((t, d), F32),
        scratch_shapes=[pltpu.VMEM((tt, d), F32)],
        compiler_params=_cparams(2, 56),
        name="peer_experts",
    )(x1b, ut, v, gm, x1, g2, b2)


def _layer(x, mem, rel_bias, w_in, b_gate, w_mem_kv, sinks, w_a, w_b, w_c, w_out, g1, b1,
           w_query, sub_keys, u_tab, v_tab, g2, b2, alpha):
    bsz, s, d = x.shape
    t = bsz * s

    def a_cols(g):
        return [w_in[:, part * A_WIDTH + g * A_GROUP_WIDTH:part * A_WIDTH + (g + 1) * A_GROUP_WIDTH]
                for part in range(3)]

    rest_at = 3 * A_WIDTH
    gates_at = rest_at + B_Q_WIDTH + 2 * B_KV_WIDTH + C_WIDTH
    w_nat = jnp.concatenate([w_in[:, gates_at:], *a_cols(0), w_in[:, rest_at:gates_at]],
                            axis=1).astype(BF16)
    dils = [dil for _, dil in A_PAIRS[1:]]
    w_dil = [jnp.concatenate(a_cols(g), axis=1).astype(BF16) for g in range(1, A_GROUPS)]
    row_w = w_nat.shape[1]
    col_aq = N_GATES * d
    col_bq = col_aq + 3 * A_GROUP_WIDTH
    col_bk = col_bq + B_Q_WIDTH
    col_bv = col_bk + B_KV_WIDTH
    col_cq = col_bv + B_KV_WIDTH

    x2 = x.reshape(t, d)
    h2, *h_dil = _in_proj(x2, bsz, w_nat, w_dil, dils, 256)
    h3 = h2.reshape(bsz, s, row_w)

    bias = _bias_tiles(rel_bias)
    groups = []
    for g in range(A_GROUPS):
        if g == 0:
            hr, c0 = h3.reshape(bsz, 1, s, row_w), col_aq // A_GROUP_WIDTH
        else:
            hr, c0 = h_dil[g - 1], 0
        groups += _attn_a_group(hr, bias, g, c0, c0 + 1, c0 + 2)

    yb = _attn_b(h3, bias, sinks, col_bq, col_bk, col_bv).reshape(t, B_Q_WIDTH)

    m = mem.shape[1]
    kvm = _project(mem.reshape(bsz * m, d), w_mem_kv.astype(BF16), 256, 2 * C_WIDTH, "mem_kv")
    yc = _attn_c(h3, kvm.reshape(bsz, m, 2 * C_WIDTH), col_cq, 512).reshape(t, C_WIDTH)

    x1, x1b, q = _merge(alpha, groups, yb, yc, h2, x2, b_gate.reshape(1, -1).astype(F32),
                        w_a.astype(BF16), w_b.astype(BF16), w_c.astype(BF16), w_out.astype(BF16),
                        g1.reshape(1, d).astype(F32), b1.reshape(1, d).astype(F32),
                        w_query.astype(BF16), 256)

    ei, ej, gate = _retrieve(q, sub_keys.astype(BF16), 256)
    gm = _gate_matrix(ei, ej, gate, 256)
    out = _experts(alpha, x1b, u_tab.astype(BF16).T, v_tab.astype(BF16), gm, x1,
                   g2.reshape(1, d).astype(F32), b2.reshape(1, d).astype(F32), 1024, 1024, 256)
    return out.reshape(bsz, s, d)


def kernel(x, mem, rel_bias, w_in, b_gate, w_mem_kv, sinks, w_branch_a, w_branch_b, w_branch_c,
           w_out, ln1_g, ln1_b, peer_w_query, peer_sub_keys, peer_u, peer_v, ln2_g, ln2_b):
    depth = w_in.shape[0]
    alpha = (2.0 * depth) ** 0.25
    for l in range(depth):
        x = _layer(x, mem, rel_bias, w_in[l], b_gate[l], w_mem_kv[l], sinks[l], w_branch_a[l],
                   w_branch_b[l], w_branch_c[l], w_out[l], ln1_g[l], ln1_b[l], peer_w_query[l],
                   peer_sub_keys[l], peer_u[l], peer_v[l], ln2_g[l], ln2_b[l], alpha)
    return x
```

```python
import functools
import math

import numpy as np
import jax
import jax.numpy as jnp
from jax import lax
from jax.experimental import pallas as pl
from jax.experimental.pallas import tpu as pltpu

F32 = jnp.float32
BF16 = jnp.bfloat16

HEAD_DIM = 64
BLOCK = 128
A_PAIRS = ((128, 1), (512, 4), (2048, 16))
A_GROUPS = 3
A_HEADS = 4
A_GROUP_WIDTH = A_HEADS * HEAD_DIM
A_WIDTH = A_GROUPS * A_GROUP_WIDTH
B_Q_HEADS = 8
B_KV_HEADS = 2
B_WINDOW = 128
B_Q_WIDTH = B_Q_HEADS * HEAD_DIM
B_KV_WIDTH = B_KV_HEADS * HEAD_DIM
C_HEADS = 4
C_HEAD_DIM = 128
C_WIDTH = C_HEADS * C_HEAD_DIM
N_GATES = 3
N_BUCKETS = 32
MAX_DISTANCE = 2048
PEER_HEADS = 8
N_KEYS = 128
PEER_TOPK = 16
PEER_HALF = 64
LN_EPS = 1e-5
NEG = -1e30

V7X_VMEM_BYTES = 64 * 1024 * 1024
SUBLANES = 8
LANES = 128


def _cparams(n_grid, vmem_mb):
    return pltpu.CompilerParams(
        dimension_semantics=("arbitrary",) * n_grid,
        vmem_limit_bytes=vmem_mb * 1024 * 1024,
    )


def _dot(a, b):
    return jnp.dot(a, b, preferred_element_type=F32)


def _dot_nt(a, b):
    return lax.dot_general(a, b, (((1,), (1,)), ((), ())), preferred_element_type=F32)


def _layer_norm(y, g, b):
    mu = jnp.mean(y, axis=-1, keepdims=True)
    yc = y - mu
    var = jnp.mean(yc * yc, axis=-1, keepdims=True)
    return yc * lax.rsqrt(var + LN_EPS) * g + b


def _proj_kernel(x_ref, w_ref, o_ref):
    o_ref[...] = _dot(x_ref[...].astype(BF16), w_ref[...]).astype(o_ref.dtype)


def _project(x, w, tm, tn, name):
    m, k = x.shape
    n = w.shape[1]
    return pl.pallas_call(
        _proj_kernel,
        grid=(n // tn, m // tm),
        in_specs=[pl.BlockSpec((tm, k), lambda j, i: (i, 0)),
                  pl.BlockSpec((k, tn), lambda j, i: (0, j))],
        out_specs=pl.BlockSpec((tm, tn), lambda j, i: (i, j)),
        out_shape=jax.ShapeDtypeStruct((m, n), BF16),
        compiler_params=_cparams(2, 48),
        name=name,
    )(x, w)


def _in_proj_kernel(x_ref, wn_ref, w1_ref, w2_ref, p1_ref, p2_ref, hn_ref, h1_ref, h2_ref):
    tm = x_ref.shape[0]
    xb = x_ref[...].astype(BF16)
    hn_ref[...] = _dot(xb, wn_ref[...]).astype(hn_ref.dtype)
    for w_ref, p_ref, h_ref in ((w1_ref, p1_ref, h1_ref), (w2_ref, p2_ref, h2_ref)):
        r = h_ref.shape[1]
        h = _dot(xb, w_ref[...]).astype(BF16)
        hp = _dot(p_ref[...], h).astype(h_ref.dtype)
        h_ref[0] = hp.reshape(r, tm // r, hp.shape[-1])


def _class_major_permutation(tm, r):
    dst = np.arange(tm)
    src = (dst % (tm // r)) * r + dst // (tm // r)
    p = np.zeros((tm, tm), np.float32)
    p[dst, src] = 1.0
    return jnp.asarray(p, BF16)


def _in_proj(x2, bsz, w_nat, w_dil, dils, tm):
    t, d = x2.shape
    s = t // bsz
    nt = s // tm
    full = lambda a: pl.BlockSpec(a.shape, lambda b, i: (0, 0))
    dil_spec = lambda r, w: pl.BlockSpec((1, r, tm // r, w), lambda b, i: (b, 0, i, 0))
    perms = [_class_major_permutation(tm, r) for r in dils]
    return pl.pallas_call(
        _in_proj_kernel,
        grid=(bsz, nt),
        in_specs=[pl.BlockSpec((tm, d), lambda b, i: (b * nt + i, 0)),
                  full(w_nat), full(w_dil[0]), full(w_dil[1]), full(perms[0]), full(perms[1])],
        out_specs=[pl.BlockSpec((tm, w_nat.shape[1]), lambda b, i: (b * nt + i, 0)),
                   dil_spec(dils[0], w_dil[0].shape[1]), dil_spec(dils[1], w_dil[1].shape[1])],
        out_shape=[jax.ShapeDtypeStruct((t, w_nat.shape[1]), BF16),
                   jax.ShapeDtypeStruct((bsz, dils[0], s // dils[0], w_dil[0].shape[1]), BF16),
                   jax.ShapeDtypeStruct((bsz, dils[1], s // dils[1], w_dil[1].shape[1]), BF16)],
        compiler_params=_cparams(2, 56),
        name="in_proj",
    )(x2, w_nat, *w_dil, *perms)


def _t5_bucket(dist):
    n = np.asarray(dist, dtype=np.int32)
    max_exact = N_BUCKETS // 2
    nf = np.maximum(n, 1).astype(np.float32)
    scale = np.float32(math.log(MAX_DISTANCE / max_exact))
    large = max_exact + (np.log(nf / np.float32(max_exact)) / scale
                         * np.float32(N_BUCKETS - max_exact)).astype(np.int32)
    large = np.minimum(large, N_BUCKETS - 1)
    return np.where(n < max_exact, n, large).astype(np.int32)


def _bucket_tiles(dist_scale, max_off):
    i = np.arange(BLOCK)[:, None]
    j = np.arange(2 * BLOCK)[None, :]
    off = BLOCK + i - j
    bucket = _t5_bucket(np.clip(off, 0, max_off) * dist_scale)
    valid = (off >= 0) & (off <= max_off)
    general = np.where(valid, bucket, -1)
    first = np.where(valid & (j >= BLOCK), bucket, -1)
    return np.stack([first, general], axis=0).astype(np.int32)


def _bias_kernel(table_ref, bucket_ref, o_ref):
    h = pl.program_id(0)
    col = jnp.where(h < B_Q_HEADS, h + A_GROUPS * A_HEADS, h - B_Q_HEADS)
    for var in range(2):
        bk = bucket_ref[0, var]
        acc = jnp.full(bk.shape, NEG, F32)
        for b in range(N_BUCKETS):
            acc = jnp.where(bk == b, table_ref[b, col], acc)
        o_ref[var, 0] = acc


def _bias_tiles(rel_bias):
    kinds = [_bucket_tiles(1, B_WINDOW - 1)] + [_bucket_tiles(d, w // d) for w, d in A_PAIRS]
    buckets = jnp.asarray(np.stack(kinds, axis=0))
    n_heads = B_Q_HEADS + A_GROUPS * A_HEADS

    def kind(h):
        return jnp.where(h < B_Q_HEADS, 0, 1 + (h - B_Q_HEADS) // A_HEADS)

    return pl.pallas_call(
        _bias_kernel,
        grid=(n_heads,),
        in_specs=[pl.BlockSpec(memory_space=pltpu.SMEM),
                  pl.BlockSpec((1, 2, BLOCK, 2 * BLOCK), lambda h: (kind(h), 0, 0, 0))],
        out_specs=pl.BlockSpec((2, 1, BLOCK, 2 * BLOCK), lambda h: (0, h, 0, 0)),
        out_shape=jax.ShapeDtypeStruct((2, n_heads, BLOCK, 2 * BLOCK), F32),
        compiler_params=_cparams(1, 32),
        name="bias_tiles",
    )(rel_bias.astype(F32), buckets)


def _attn_a_kernel(r, q_ref, kp_ref, kc_ref, vp_ref, vc_ref, bias_ref, o0_ref, o1_ref, l0_ref, l1_ref):
    c = pl.program_id(2)
    q = q_ref[0, 0]
    kcat = jnp.concatenate([kp_ref[0, 0], kc_ref[0, 0]], axis=0)
    vcat = jnp.concatenate([vp_ref[0, 0], vc_ref[0, 0]], axis=0)
    scale = HEAD_DIM ** -0.5
    rows = pl.ds(c, BLOCK, stride=r) if r > 1 else slice(None)
    heads = [slice(HEAD_DIM * h, HEAD_DIM * (h + 1)) for h in range(A_HEADS)]
    logits = jnp.concatenate([_dot_nt(q[:, sl], kcat[:, sl]) for sl in heads], axis=0)
    logits = logits * scale + bias_ref[0].reshape(A_HEADS * BLOCK, 2 * BLOCK)
    m = jnp.max(logits, axis=-1, keepdims=True)
    e = jnp.exp(logits - m)
    s = jnp.sum(e, axis=-1, keepdims=True)
    p = (e * (1.0 / s)).astype(BF16)
    lse = m + jnp.log(s)
    outs = [_dot(p[BLOCK * h:BLOCK * (h + 1)], vcat[:, sl]) for h, sl in enumerate(heads)]
    lses = [jnp.broadcast_to(lse[BLOCK * h:BLOCK * (h + 1)], (BLOCK, HEAD_DIM))
            for h in range(A_HEADS)]
    o0_ref[rows, :] = jnp.concatenate(outs[:2], axis=1)
    o1_ref[rows, :] = jnp.concatenate(outs[2:], axis=1)
    l0_ref[rows, :] = jnp.concatenate(lses[:2], axis=1)
    l1_ref[rows, :] = jnp.concatenate(lses[2:], axis=1)


def _attn_a_group(hr, bias, g, col_q, col_k, col_v):
    bsz, r, ln, _ = hr.shape
    nblk = ln // BLOCK
    half = A_GROUP_WIDTH // 2

    def spec(col, prev):
        if prev:
            return pl.BlockSpec((1, 1, BLOCK, A_GROUP_WIDTH),
                                lambda b, n, c: (b, c, jnp.maximum(n - 1, 0), col))
        return pl.BlockSpec((1, 1, BLOCK, A_GROUP_WIDTH), lambda b, n, c: (b, c, n, col))

    out_spec = pl.BlockSpec((BLOCK * r, half), lambda b, n, c: (b * nblk + n, 0))
    return pl.pallas_call(
        functools.partial(_attn_a_kernel, r),
        grid=(bsz, nblk, r),
        in_specs=[spec(col_q, False), spec(col_k, True), spec(col_k, False),
                  spec(col_v, True), spec(col_v, False),
                  pl.BlockSpec((1, A_HEADS, BLOCK, 2 * BLOCK),
                               lambda b, n, c: (jnp.minimum(n, 1), B_Q_HEADS // A_HEADS + g, 0, 0))],
        out_specs=[out_spec] * 4,
        out_shape=[jax.ShapeDtypeStruct((bsz * ln * r, half), F32)] * 4,
        compiler_params=_cparams(3, 32),
        name=f"attn_a{g}",
    )(hr, hr, hr, hr, hr, bias)


def _attn_b_kernel(sinks_ref, q0_ref, q1_ref, kp_ref, kc_ref, vp_ref, vc_ref, bias_ref, o_ref):
    kcat = jnp.concatenate([kp_ref[0], kc_ref[0]], axis=0)
    vcat = jnp.concatenate([vp_ref[0], vc_ref[0]], axis=0)
    scale = HEAD_DIM ** -0.5
    per_kv = B_Q_HEADS // B_KV_HEADS
    parts, sink_rows = [], []
    for h in range(B_Q_HEADS):
        kv = h // per_kv
        q = (q0_ref if kv == 0 else q1_ref)[0]
        qsl = slice(HEAD_DIM * (h % per_kv), HEAD_DIM * (h % per_kv + 1))
        parts.append(_dot_nt(q[:, qsl], kcat[:, HEAD_DIM * kv:HEAD_DIM * (kv + 1)]))
        sink_rows.append(jnp.full((BLOCK, 1), sinks_ref[h], F32))
    logits = jnp.concatenate(parts, axis=0) * scale + bias_ref[0].reshape(B_Q_HEADS * BLOCK, 2 * BLOCK)
    sink = jnp.concatenate(sink_rows, axis=0)
    m = jnp.maximum(jnp.max(logits, axis=-1, keepdims=True), sink)
    e = jnp.exp(logits - m)
    denom = jnp.sum(e, axis=-1, keepdims=True) + jnp.exp(sink - m)
    p = (e * (1.0 / denom)).astype(BF16)
    outs = [_dot(p[BLOCK * h:BLOCK * (h + 1)],
                 vcat[:, HEAD_DIM * (h // per_kv):HEAD_DIM * (h // per_kv + 1)])
            for h in range(B_Q_HEADS)]
    o_ref[0] = jnp.concatenate(outs, axis=1).astype(o_ref.dtype)


def _attn_b(h3, bias, sinks, col_q, col_k, col_v):
    bsz, s, _ = h3.shape
    nb = s // BLOCK
    qw = B_Q_WIDTH // B_KV_HEADS
    cq = col_q // qw
    ck = col_k // B_KV_WIDTH
    cv = col_v // B_KV_WIDTH

    def kv_spec(col, prev):
        if prev:
            return pl.BlockSpec((1, BLOCK, B_KV_WIDTH), lambda b, n: (b, jnp.maximum(n - 1, 0), col))
        return pl.BlockSpec((1, BLOCK, B_KV_WIDTH), lambda b, n: (b, n, col))

    return pl.pallas_call(
        _attn_b_kernel,
        grid=(bsz, nb),
        in_specs=[pl.BlockSpec(memory_space=pltpu.SMEM),
                  pl.BlockSpec((1, BLOCK, qw), lambda b, n: (b, n, cq)),
                  pl.BlockSpec((1, BLOCK, qw), lambda b, n: (b, n, cq + 1)),
                  kv_spec(ck, True), kv_spec(ck, False), kv_spec(cv, True), kv_spec(cv, False),
                  pl.BlockSpec((1, B_Q_HEADS, BLOCK, 2 * BLOCK),
                               lambda b, n: (jnp.minimum(n, 1), 0, 0, 0))],
        out_specs=pl.BlockSpec((1, BLOCK, B_Q_WIDTH), lambda b, n: (b, n, 0)),
        out_shape=jax.ShapeDtypeStruct((bsz, s, B_Q_WIDTH), BF16),
        compiler_params=_cparams(2, 32),
        name="attn_b",
    )(sinks.astype(F32), h3, h3, h3, h3, h3, h3, bias)


def _attn_c_kernel(q_ref, k_ref, v_ref, o_ref):
    q = q_ref[0]
    k = k_ref[0]
    v = v_ref[0]
    scale = C_HEAD_DIM ** -0.5
    for h in range(C_HEADS):
        sl = slice(C_HEAD_DIM * h, C_HEAD_DIM * (h + 1))
        logits = _dot_nt(q[:, sl], k[:, sl]) * scale
        m = jnp.max(logits, axis=-1, keepdims=True)
        e = jnp.exp(logits - m)
        inv = 1.0 / jnp.sum(e, axis=-1, keepdims=True)
        o_ref[0, :, sl] = _dot((e * inv).astype(BF16), v[:, sl]).astype(o_ref.dtype)


def _attn_c(h3, kvm, col_q, tq):
    bsz, s, _ = h3.shape
    m = kvm.shape[1]
    cq = col_q // C_WIDTH
    return pl.pallas_call(
        _attn_c_kernel,
        grid=(bsz, s // tq),
        in_specs=[pl.BlockSpec((1, tq, C_WIDTH), lambda b, n: (b, n, cq)),
                  pl.BlockSpec((1, m, C_WIDTH), lambda b, n: (b, 0, 0)),
                  pl.BlockSpec((1, m, C_WIDTH), lambda b, n: (b, 0, 1))],
        out_specs=pl.BlockSpec((1, tq, C_WIDTH), lambda b, n: (b, n, 0)),
        out_shape=jax.ShapeDtypeStruct((bsz, s, C_WIDTH), BF16),
        compiler_params=_cparams(2, 32),
        name="attn_c",
    )(h3, kvm, kvm)


def _merge_kernel(alpha, *refs):
    group_refs, refs = refs[:4 * A_GROUPS], refs[4 * A_GROUPS:]
    (yb_ref, yc_ref, gate_ref, x_ref, bg_ref, wa_ref, wb_ref, wc_ref, wo_ref, g1_ref, b1_ref,
     wq_ref, x1_ref, x1b_ref, q_ref) = refs
    d = x_ref.shape[-1]
    halves = []
    for hf in range(2):
        o0, o1, o2 = (group_refs[4 * g + hf][...] for g in range(A_GROUPS))
        la, lb, lc = (group_refs[4 * g + 2 + hf][...] for g in range(A_GROUPS))
        m = jnp.maximum(jnp.maximum(la, lb), lc)
        wa, wb, wc = jnp.exp(la - m), jnp.exp(lb - m), jnp.exp(lc - m)
        inv = 1.0 / (wa + wb + wc)
        halves.append((wa * inv) * o0 + (wb * inv) * o1 + (wc * inv) * o2)
    ya = jnp.concatenate(halves, axis=1)

    def gate(n):
        pre = gate_ref[:, n * d:(n + 1) * d].astype(F32) + bg_ref[:, n * d:(n + 1) * d]
        return jax.nn.sigmoid(pre)

    merged = (gate(0) * _dot(ya.astype(BF16), wa_ref[...])
              + gate(1) * _dot(yb_ref[...], wb_ref[...])
              + gate(2) * _dot(yc_ref[...], wc_ref[...]))
    y = alpha * x_ref[...] + _dot(merged.astype(BF16), wo_ref[...])
    x1 = _layer_norm(y, g1_ref[...], b1_ref[...])
    x1_ref[...] = x1
    x1b = x1.astype(BF16)
    x1b_ref[...] = x1b
    q_ref[...] = _dot(x1b, wq_ref[...]).astype(q_ref.dtype)


def _merge(alpha, groups, yb, yc, h2, x2, b_gate, w_a, w_b, w_c, w_o, g1, b1, w_q, tm):
    t, d = x2.shape
    row = lambda w: pl.BlockSpec((tm, w), lambda i: (i, 0))
    full = lambda a: pl.BlockSpec(a.shape, lambda i: (0,) * a.ndim)
    weights = [b_gate, w_a, w_b, w_c, w_o, g1, b1, w_q]
    ins = [*groups, yb, yc, h2, x2, *weights]
    in_specs = ([row(A_GROUP_WIDTH // 2)] * len(groups)
                + [row(B_Q_WIDTH), row(C_WIDTH), row(N_GATES * d), row(d)]
                + [full(a) for a in weights])
    return pl.pallas_call(
        functools.partial(_merge_kernel, alpha),
        grid=(t // tm,),
        in_specs=in_specs,
        out_specs=[row(d), row(d), row(w_q.shape[1])],
        out_shape=[jax.ShapeDtypeStruct((t, d), F32), jax.ShapeDtypeStruct((t, d), BF16),
                   jax.ShapeDtypeStruct((t, w_q.shape[1]), BF16)],
        compiler_params=_cparams(1, 48),
        name="merge_ln1",
    )(*ins)


def _extract_top(vals, payloads, k):
    n = vals.shape[0]
    row = lax.broadcasted_iota(jnp.int32, vals.shape, 0).astype(F32)
    tops, rows = [], []
    picked = [[] for _ in payloads]
    for _ in range(k):
        m = jnp.max(vals, axis=0, keepdims=True)
        first = jnp.min(jnp.where(vals == m, row, float(n)), axis=0, keepdims=True)
        sel = row == first
        tops.append(m)
        rows.append(first)
        for lst, p in zip(picked, payloads):
            lst.append(jnp.max(jnp.where(sel, p, -1.0), axis=0, keepdims=True))
        vals = jnp.where(sel, -jnp.inf, vals)
    return (jnp.concatenate(tops, axis=0), jnp.concatenate(rows, axis=0),
            [jnp.concatenate(lst, axis=0) for lst in picked])


def _retrieve_kernel(q_ref, keys_ref, i_ref, j_ref, g_ref):
    tt = q_ref.shape[0]
    q = q_ref[...]
    out_i, out_j, out_g = [], [], []
    for h in range(PEER_HEADS):
        v, ix = [], []
        for c in range(2):
            col = (2 * h + c) * PEER_HALF
            scores = _dot_nt(keys_ref[h, c], q[:, col:col + PEER_HALF])
            tv, ti, _ = _extract_top(scores, [], PEER_TOPK)
            v.append(tv)
            ix.append(ti)
        cv, ci, cj = [], [], []
        for k1 in range(PEER_TOPK):
            n2 = PEER_TOPK // (k1 + 1)
            cv.append(v[0][k1:k1 + 1] + v[1][:n2])
            ci.append(jnp.broadcast_to(ix[0][k1:k1 + 1], (n2, tt)))
            cj.append(ix[1][:n2])
        cv = jnp.concatenate(cv, axis=0)
        ci = jnp.concatenate(ci, axis=0)
        cj = jnp.concatenate(cj, axis=0)
        top, _, (ei, ej) = _extract_top(cv, [ci, cj], PEER_TOPK)
        e = jnp.exp(top - top[0:1])
        out_g.append(e * (1.0 / jnp.sum(e, axis=0, keepdims=True)))
        out_i.append(ei)
        out_j.append(ej)
    i_ref[...] = jnp.concatenate(out_i, axis=0).T
    j_ref[...] = jnp.concatenate(out_j, axis=0).T
    g_ref[...] = jnp.concatenate(out_g, axis=0).T


def _retrieve(q, keys, tt):
    t, w = q.shape
    npair = PEER_HEADS * PEER_TOPK
    out = pl.BlockSpec((tt, npair), lambda i: (i, 0))
    return pl.pallas_call(
        _retrieve_kernel,
        grid=(t // tt,),
        in_specs=[pl.BlockSpec((tt, w), lambda i: (i, 0)),
                  pl.BlockSpec(keys.shape, lambda i: (0, 0, 0, 0))],
        out_specs=[out, out, out],
        out_shape=[jax.ShapeDtypeStruct((t, npair), F32)] * 3,
        compiler_params=_cparams(1, 32),
        name="peer_retrieve",
    )(q, keys)


GATE_TOKENS_PER_TRIP = 64


def _gate_matrix_kernel(i_ref, j_ref, g_ref, o_ref):
    tt, npair = i_ref.shape
    key = lax.broadcasted_iota(jnp.int32, (N_KEYS, npair), 0).astype(F32).astype(BF16)
    one = jnp.ones((N_KEYS, npair), BF16)
    zero = jnp.zeros((N_KEYS, npair), BF16)

    def per_trip(trip, carry):
        for u in range(GATE_TOKENS_PER_TRIP):
            t = trip * GATE_TOKENS_PER_TRIP + u
            irow = jnp.broadcast_to(i_ref[pl.ds(t, 1), :], (N_KEYS, npair)).astype(BF16)
            jrow = jnp.broadcast_to(j_ref[pl.ds(t, 1), :], (N_KEYS, npair)).astype(BF16)
            grow = jnp.broadcast_to(g_ref[pl.ds(t, 1), :], (N_KEYS, npair)).astype(BF16)
            a = jnp.where(key == irow, grow, zero)
            b = jnp.where(key == jrow, one, zero)
            grp, s = divmod(u, SUBLANES)
            start = (trip * (GATE_TOKENS_PER_TRIP // SUBLANES) + grp) * (SUBLANES * N_KEYS) + s
            o_ref[pl.ds(start, N_KEYS, stride=SUBLANES), :] = _dot_nt(a, b)
        return carry

    lax.fori_loop(0, tt // GATE_TOKENS_PER_TRIP, per_trip, 0)


def _gate_matrix(ei, ej, gate, tt):
    t, npair = ei.shape
    spec = pl.BlockSpec((tt, npair), lambda i: (i, 0))
    gm = pl.pallas_call(
        _gate_matrix_kernel,
        grid=(t // tt,),
        in_specs=[spec, spec, spec],
        out_specs=pl.BlockSpec((tt * N_KEYS, N_KEYS), lambda i: (i, 0)),
        out_shape=jax.ShapeDtypeStruct((t * N_KEYS, N_KEYS), F32),
        compiler_params=_cparams(1, 48),
        name="peer_gate_matrix",
    )(ei, ej, gate)
    return gm.reshape(t // SUBLANES, N_KEYS, SUBLANES, N_KEYS)


def _experts_kernel(alpha, sub, xb_ref, ut_ref, v_ref, gm_ref, x1_ref, g2_ref, b2_ref, o_ref, acc_ref):
    j = pl.program_id(1)

    @pl.when(j == 0)
    def _():
        acc_ref[...] = jnp.zeros_like(acc_ref)

    tt = xb_ref.shape[0]
    n_i = gm_ref.shape[1]
    for r0 in range(0, tt, sub):
        hid = _dot(xb_ref[r0:r0 + sub, :], ut_ref[...])
        act = 0.5 * hid * (1.0 + lax.erf(hid * (2.0 ** -0.5)))
        grp = slice(r0 // SUBLANES, (r0 + sub) // SUBLANES)
        gate = jnp.concatenate([gm_ref[grp, i].reshape(sub, N_KEYS) for i in range(n_i)], axis=1)
        p = (gate * act).astype(BF16)
        acc_ref[r0:r0 + sub, :] += _dot(p, v_ref[...])

    @pl.when(j == pl.num_programs(1) - 1)
    def _():
        y = alpha * x1_ref[...] + acc_ref[...]
        o_ref[...] = _layer_norm(y, g2_ref[...], b2_ref[...])


def _experts(alpha, x1b, ut, v, gm, x1, g2, b2, tt, ec, sub):
    t, d = x1.shape
    ne = ut.shape[1]
    return pl.pallas_call(
        functools.partial(_experts_kernel, alpha, sub),
        grid=(t // tt, ne // ec),
        in_specs=[pl.BlockSpec((tt, d), lambda i, j: (i, 0)),
                  pl.BlockSpec((d, ec), lambda i, j: (0, j)),
                  pl.BlockSpec((ec, d), lambda i, j: (j, 0)),
                  pl.BlockSpec((tt // SUBLANES, ec // N_KEYS, SUBLANES, N_KEYS),
                               lambda i, j: (i, j, 0, 0)),
                  pl.BlockSpec((tt, d), lambda i, j: (i, 0)),
                  pl.BlockSpec((1, d), lambda i, j: (0, 0)),
                  pl.BlockSpec((1, d), lambda i, j: (0, 0))],
        out_specs=pl.BlockSpec((tt, d), lambda i, j: (i, 0)),
        out_shape=jax.ShapeDtypeStruct((t, d), F32),
        scratch_shapes=[pltpu.VMEM((tt, d), F32)],
        compiler_params=_cparams(2, 56),
        name="peer_experts",
    )(x1b, ut, v, gm, x1, g2, b2)


def _layer(x, mem, rel_bias, w_in, b_gate, w_mem_kv, sinks, w_a, w_b, w_c, w_out, g1, b1,
           w_query, sub_keys, u_tab, v_tab, g2, b2, alpha):
    bsz, s, d = x.shape
    t = bsz * s

    def a_cols(g):
        return [w_in[:, part * A_WIDTH + g * A_GROUP_WIDTH:part * A_WIDTH + (g + 1) * A_GROUP_WIDTH]
                for part in range(3)]

    rest_at = 3 * A_WIDTH
    gates_at = rest_at + B_Q_WIDTH + 2 * B_KV_WIDTH + C_WIDTH
    w_nat = jnp.concatenate([w_in[:, gates_at:], *a_cols(0), w_in[:, rest_at:gates_at]],
                            axis=1).astype(BF16)
    dils = [dil for _, dil in A_PAIRS[1:]]
    w_dil = [jnp.concatenate(a_cols(g), axis=1).astype(BF16) for g in range(1, A_GROUPS)]
    row_w = w_nat.shape[1]
    col_aq = N_GATES * d
    col_bq = col_aq + 3 * A_GROUP_WIDTH
    col_bk = col_bq + B_Q_WIDTH
    col_bv = col_bk + B_KV_WIDTH
    col_cq = col_bv + B_KV_WIDTH

    x2 = x.reshape(t, d)
    h2, *h_dil = _in_proj(x2, bsz, w_nat, w_dil, dils, 256)
    h3 = h2.reshape(bsz, s, row_w)

    bias = _bias_tiles(rel_bias)
    groups = []
    for g in range(A_GROUPS):
        if g == 0:
            hr, c0 = h3.reshape(bsz, 1, s, row_w), col_aq // A_GROUP_WIDTH
        else:
            hr, c0 = h_dil[g - 1], 0
        groups += _attn_a_group(hr, bias, g, c0, c0 + 1, c0 + 2)

    yb = _attn_b(h3, bias, sinks, col_bq, col_bk, col_bv).reshape(t, B_Q_WIDTH)

    m = mem.shape[1]
    kvm = _project(mem.reshape(bsz * m, d), w_mem_kv.astype(BF16), 256, 2 * C_WIDTH, "mem_kv")
    yc = _attn_c(h3, kvm.reshape(bsz, m, 2 * C_WIDTH), col_cq, 512).reshape(t, C_WIDTH)

    x1, x1b, q = _merge(alpha, groups, yb, yc, h2, x2, b_gate.reshape(1, -1).astype(F32),
                        w_a.astype(BF16), w_b.astype(BF16), w_c.astype(BF16), w_out.astype(BF16),
                        g1.reshape(1, d).astype(F32), b1.reshape(1, d).astype(F32),
                        w_query.astype(BF16), 256)

    ei, ej, gate = _retrieve(q, sub_keys.astype(BF16), 256)
    gm = _gate_matrix(ei, ej, gate, 256)
    out = _experts(alpha, x1b, u_tab.astype(BF16).T, v_tab.astype(BF16), gm, x1,
                   g2.reshape(1, d).astype(F32), b2.reshape(1, d).astype(F32), 1024, 1024, 256)
    return out.reshape(bsz, s, d)


def kernel(x, mem, rel_bias, w_in, b_gate, w_mem_kv, sinks, w_branch_a, w_branch_b, w_branch_c,
           w_out, ln1_g, ln1_b, peer_w_query, peer_sub_keys, peer_u, peer_v, ln2_g, ln2_b):
    depth = w_in.shape[0]
    alpha = (2.0 * depth) ** 0.25
    for l in range(depth):
        x = _layer(x, mem, rel_bias, w_in[l], b_gate[l], w_mem_kv[l], sinks[l], w_branch_a[l],
                   w_branch_b[l], w_branch_c[l], w_out[l], ln1_g[l], ln1_b[l], peer_w_query[l],
                   peer_sub_keys[l], peer_u[l], peer_v[l], ln2_g[l], ln2_b[l], alpha)
    return x
```

```python
import functools
import math

import numpy as np
import jax
import jax.numpy as jnp
from jax import lax
from jax.experimental import pallas as pl
from jax.experimental.pallas import tpu as pltpu

F32 = jnp.float32
BF16 = jnp.bfloat16

HEAD_DIM = 64
BLOCK = 128
A_PAIRS = ((128, 1), (512, 4), (2048, 16))
A_GROUPS = 3
A_HEADS = 4
A_GROUP_WIDTH = A_HEADS * HEAD_DIM
A_WIDTH = A_GROUPS * A_GROUP_WIDTH
B_Q_HEADS = 8
B_KV_HEADS = 2
B_WINDOW = 128
B_Q_WIDTH = B_Q_HEADS * HEAD_DIM
B_KV_WIDTH = B_KV_HEADS * HEAD_DIM
C_HEADS = 4
C_HEAD_DIM = 128
C_WIDTH = C_HEADS * C_HEAD_DIM
N_GATES = 3
N_BUCKETS = 32
MAX_DISTANCE = 2048
PEER_HEADS = 8
N_KEYS = 128
PEER_TOPK = 16
PEER_HALF = 64
LN_EPS = 1e-5
NEG = -1e30

V7X_VMEM_BYTES = 64 * 1024 * 1024
SUBLANES = 8
LANES = 128


def _cparams(n_grid, vmem_mb):
    return pltpu.CompilerParams(
        dimension_semantics=("arbitrary",) * n_grid,
        vmem_limit_bytes=vmem_mb * 1024 * 1024,
    )


def _dot(a, b):
    return jnp.dot(a, b, preferred_element_type=F32)


def _dot_nt(a, b):
    return lax.dot_general(a, b, (((1,), (1,)), ((), ())), preferred_element_type=F32)


def _layer_norm(y, g, b):
    mu = jnp.mean(y, axis=-1, keepdims=True)
    yc = y - mu
    var = jnp.mean(yc * yc, axis=-1, keepdims=True)
    return yc * lax.rsqrt(var + LN_EPS) * g + b


def _proj_kernel(x_ref, w_ref, o_ref):
    o_ref[...] = _dot(x_ref[...].astype(BF16), w_ref[...]).astype(o_ref.dtype)


def _project(x, w, tm, tn, name):
    m, k = x.shape
    n = w.shape[1]
    return pl.pallas_call(
        _proj_kernel,
        grid=(n // tn, m // tm),
        in_specs=[pl.BlockSpec((tm, k), lambda j, i: (i, 0)),
                  pl.BlockSpec((k, tn), lambda j, i: (0, j))],
        out_specs=pl.BlockSpec((tm, tn), lambda j, i: (i, j)),
        out_shape=jax.ShapeDtypeStruct((m, n), BF16),
        compiler_params=_cparams(2, 48),
        name=name,
    )(x, w)


def _in_proj_kernel(x_ref, wn_ref, w1_ref, w2_ref, p1_ref, p2_ref, hn_ref, h1_ref, h2_ref):
    tm = x_ref.shape[0]
    xb = x_ref[...].astype(BF16)
    hn_ref[...] = _dot(xb, wn_ref[...]).astype(hn_ref.dtype)
    for w_ref, p_ref, h_ref in ((w1_ref, p1_ref, h1_ref), (w2_ref, p2_ref, h2_ref)):
        r = h_ref.shape[1]
        h = _dot(xb, w_ref[...]).astype(BF16)
        hp = _dot(p_ref[...], h).astype(h_ref.dtype)
        h_ref[0] = hp.reshape(r, tm // r, hp.shape[-1])


def _class_major_permutation(tm, r):
    dst = np.arange(tm)
    src = (dst % (tm // r)) * r + dst // (tm // r)
    p = np.zeros((tm, tm), np.float32)
    p[dst, src] = 1.0
    return jnp.asarray(p, BF16)


def _in_proj(x2, bsz, w_nat, w_dil, dils, tm):
    t, d = x2.shape
    s = t // bsz
    nt = s // tm
    full = lambda a: pl.BlockSpec(a.shape, lambda b, i: (0, 0))
    dil_spec = lambda r, w: pl.BlockSpec((1, r, tm // r, w), lambda b, i: (b, 0, i, 0))
    perms = [_class_major_permutation(tm, r) for r in dils]
    return pl.pallas_call(
        _in_proj_kernel,
        grid=(bsz, nt),
        in_specs=[pl.BlockSpec((tm, d), lambda b, i: (b * nt + i, 0)),
                  full(w_nat), full(w_dil[0]), full(w_dil[1]), full(perms[0]), full(perms[1])],
        out_specs=[pl.BlockSpec((tm, w_nat.shape[1]), lambda b, i: (b * nt + i, 0)),
                   dil_spec(dils[0], w_dil[0].shape[1]), dil_spec(dils[1], w_dil[1].shape[1])],
        out_shape=[jax.ShapeDtypeStruct((t, w_nat.shape[1]), BF16),
                   jax.ShapeDtypeStruct((bsz, dils[0], s // dils[0], w_dil[0].shape[1]), BF16),
                   jax.ShapeDtypeStruct((bsz, dils[1], s // dils[1], w_dil[1].shape[1]), BF16)],
        compiler_params=_cparams(2, 56),
        name="in_proj",
    )(x2, w_nat, *w_dil, *perms)


def _t5_bucket(dist):
    n = np.asarray(dist, dtype=np.int32)
    max_exact = N_BUCKETS // 2
    nf = np.maximum(n, 1).astype(np.float32)
    scale = np.float32(math.log(MAX_DISTANCE / max_exact))
    large = max_exact + (np.log(nf / np.float32(max_exact)) / scale
                         * np.float32(N_BUCKETS - max_exact)).astype(np.int32)
    large = np.minimum(large, N_BUCKETS - 1)
    return np.where(n < max_exact, n, large).astype(np.int32)


def _bucket_tiles(dist_scale, max_off):
    i = np.arange(BLOCK)[:, None]
    j = np.arange(2 * BLOCK)[None, :]
    off = BLOCK + i - j
    bucket = _t5_bucket(np.clip(off, 0, max_off) * dist_scale)
    valid = (off >= 0) & (off <= max_off)
    general = np.where(valid, bucket, -1)
    first = np.where(valid & (j >= BLOCK), bucket, -1)
    return np.stack([first, general], axis=0).astype(np.int32)


def _bias_kernel(table_ref, bucket_ref, o_ref):
    h = pl.program_id(0)
    col = jnp.where(h < B_Q_HEADS, h + A_GROUPS * A_HEADS, h - B_Q_HEADS)
    for var in range(2):
        bk = bucket_ref[0, var]
        acc = jnp.full(bk.shape, NEG, F32)
        for b in range(N_BUCKETS):
            acc = jnp.where(bk == b, table_ref[b, col], acc)
        o_ref[var, 0] = acc


def _bias_tiles(rel_bias):
    kinds = [_bucket_tiles(1, B_WINDOW - 1)] + [_bucket_tiles(d, w // d) for w, d in A_PAIRS]
    buckets = jnp.asarray(np.stack(kinds, axis=0))
    n_heads = B_Q_HEADS + A_GROUPS * A_HEADS

    def kind(h):
        return jnp.where(h < B_Q_HEADS, 0, 1 + (h - B_Q_HEADS) // A_HEADS)

    return pl.pallas_call(
        _bias_kernel,
        grid=(n_heads,),
        in_specs=[pl.BlockSpec(memory_space=pltpu.SMEM),
                  pl.BlockSpec((1, 2, BLOCK, 2 * BLOCK), lambda h: (kind(h), 0, 0, 0))],
        out_specs=pl.BlockSpec((2, 1, BLOCK, 2 * BLOCK), lambda h: (0, h, 0, 0)),
        out_shape=jax.ShapeDtypeStruct((2, n_heads, BLOCK, 2 * BLOCK), F32),
        compiler_params=_cparams(1, 32),
        name="bias_tiles",
    )(rel_bias.astype(F32), buckets)


def _attn_a_kernel(r, q_ref, kp_ref, kc_ref, vp_ref, vc_ref, bias_ref, o0_ref, o1_ref, l0_ref, l1_ref):
    c = pl.program_id(2)
    q = q_ref[0, 0]
    kcat = jnp.concatenate([kp_ref[0, 0], kc_ref[0, 0]], axis=0)
    vcat = jnp.concatenate([vp_ref[0, 0], vc_ref[0, 0]], axis=0)
    scale = HEAD_DIM ** -0.5
    rows = pl.ds(c, BLOCK, stride=r) if r > 1 else slice(None)
    heads = [slice(HEAD_DIM * h, HEAD_DIM * (h + 1)) for h in range(A_HEADS)]
    logits = jnp.concatenate([_dot_nt(q[:, sl], kcat[:, sl]) for sl in heads], axis=0)
    logits = logits * scale + bias_ref[0].reshape(A_HEADS * BLOCK, 2 * BLOCK)
    m = jnp.max(logits, axis=-1, keepdims=True)
    e = jnp.exp(logits - m)
    s = jnp.sum(e, axis=-1, keepdims=True)
    p = (e * (1.0 / s)).astype(BF16)
    lse = m + jnp.log(s)
    outs = [_dot(p[BLOCK * h:BLOCK * (h + 1)], vcat[:, sl]) for h, sl in enumerate(heads)]
    lses = [jnp.broadcast_to(lse[BLOCK * h:BLOCK * (h + 1)], (BLOCK, HEAD_DIM))
            for h in range(A_HEADS)]
    o0_ref[rows, :] = jnp.concatenate(outs[:2], axis=1)
    o1_ref[rows, :] = jnp.concatenate(outs[2:], axis=1)
    l0_ref[rows, :] = jnp.concatenate(lses[:2], axis=1)
    l1_ref[rows, :] = jnp.concatenate(lses[2:], axis=1)


def _attn_a_group(hr, bias, g, col_q, col_k, col_v):
    bsz, r, ln, _ = hr.shape
    nblk = ln // BLOCK
    half = A_GROUP_WIDTH // 2

    def spec(col, prev):
        if prev:
            return pl.BlockSpec((1, 1, BLOCK, A_GROUP_WIDTH),
                                lambda b, n, c: (b, c, jnp.maximum(n - 1, 0), col))
        return pl.BlockSpec((1, 1, BLOCK, A_GROUP_WIDTH), lambda b, n, c: (b, c, n, col))

    out_spec = pl.BlockSpec((BLOCK * r, half), lambda b, n, c: (b * nblk + n, 0))
    return pl.pallas_call(
        functools.partial(_attn_a_kernel, r),
        grid=(bsz, nblk, r),
        in_specs=[spec(col_q, False), spec(col_k, True), spec(col_k, False),
                  spec(col_v, True), spec(col_v, False),
                  pl.BlockSpec((1, A_HEADS, BLOCK, 2 * BLOCK),
                               lambda b, n, c: (jnp.minimum(n, 1), B_Q_HEADS // A_HEADS + g, 0, 0))],
        out_specs=[out_spec] * 4,
        out_shape=[jax.ShapeDtypeStruct((bsz * ln * r, half), F32)] * 4,
        compiler_params=_cparams(3, 32),
        name=f"attn_a{g}",
    )(hr, hr, hr, hr, hr, bias)


def _attn_b_kernel(sinks_ref, q0_ref, q1_ref, kp_ref, kc_ref, vp_ref, vc_ref, bias_ref, o_ref):
    kcat = jnp.concatenate([kp_ref[0], kc_ref[0]], axis=0)
    vcat = jnp.concatenate([vp_ref[0], vc_ref[0]], axis=0)
    scale = HEAD_DIM ** -0.5
    per_kv = B_Q_HEADS // B_KV_HEADS
    parts, sink_rows = [], []
    for h in range(B_Q_HEADS):
        kv = h // per_kv
        q = (q0_ref if kv == 0 else q1_ref)[0]
        qsl = slice(HEAD_DIM * (h % per_kv), HEAD_DIM * (h % per_kv + 1))
        parts.append(_dot_nt(q[:, qsl], kcat[:, HEAD_DIM * kv:HEAD_DIM * (kv + 1)]))
        sink_rows.append(jnp.full((BLOCK, 1), sinks_ref[h], F32))
    logits = jnp.concatenate(parts, axis=0) * scale + bias_ref[0].reshape(B_Q_HEADS * BLOCK, 2 * BLOCK)
    sink = jnp.concatenate(sink_rows, axis=0)
    m = jnp.maximum(jnp.max(logits, axis=-1, keepdims=True), sink)
    e = jnp.exp(logits - m)
    denom = jnp.sum(e, axis=-1, keepdims=True) + jnp.exp(sink - m)
    p = (e * (1.0 / denom)).astype(BF16)
    outs = [_dot(p[BLOCK * h:BLOCK * (h + 1)],
                 vcat[:, HEAD_DIM * (h // per_kv):HEAD_DIM * (h // per_kv + 1)])
            for h in range(B_Q_HEADS)]
    o_ref[0] = jnp.concatenate(outs, axis=1).astype(o_ref.dtype)


def _attn_b(h3, bias, sinks, col_q, col_k, col_v):
    bsz, s, _ = h3.shape
    nb = s // BLOCK
    qw = B_Q_WIDTH // B_KV_HEADS
    cq = col_q // qw
    ck = col_k // B_KV_WIDTH
    cv = col_v // B_KV_WIDTH

    def kv_spec(col, prev):
        if prev:
            return pl.BlockSpec((1, BLOCK, B_KV_WIDTH), lambda b, n: (b, jnp.maximum(n - 1, 0), col))
        return pl.BlockSpec((1, BLOCK, B_KV_WIDTH), lambda b, n: (b, n, col))

    return pl.pallas_call(
        _attn_b_kernel,
        grid=(bsz, nb),
        in_specs=[pl.BlockSpec(memory_space=pltpu.SMEM),
                  pl.BlockSpec((1, BLOCK, qw), lambda b, n: (b, n, cq)),
                  pl.BlockSpec((1, BLOCK, qw), lambda b, n: (b, n, cq + 1)),
                  kv_spec(ck, True), kv_spec(ck, False), kv_spec(cv, True), kv_spec(cv, False),
                  pl.BlockSpec((1, B_Q_HEADS, BLOCK, 2 * BLOCK),
                               lambda b, n: (jnp.minimum(n, 1), 0, 0, 0))],
        out_specs=pl.BlockSpec((1, BLOCK, B_Q_WIDTH), lambda b, n: (b, n, 0)),
        out_shape=jax.ShapeDtypeStruct((bsz, s, B_Q_WIDTH), BF16),
        compiler_params=_cparams(2, 32),
        name="attn_b",
    )(sinks.astype(F32), h3, h3, h3, h3, h3, h3, bias)


def _attn_c_kernel(q_ref, k_ref, v_ref, o_ref):
    q = q_ref[0]
    k = k_ref[0]
    v = v_ref[0]
    scale = C_HEAD_DIM ** -0.5
    for h in range(C_HEADS):
        sl = slice(C_HEAD_DIM * h, C_HEAD_DIM * (h + 1))
        logits = _dot_nt(q[:, sl], k[:, sl]) * scale
        m = jnp.max(logits, axis=-1, keepdims=True)
        e = jnp.exp(logits - m)
        inv = 1.0 / jnp.sum(e, axis=-1, keepdims=True)
        o_ref[0, :, sl] = _dot((e * inv).astype(BF16), v[:, sl]).astype(o_ref.dtype)


def _attn_c(h3, kvm, col_q, tq):
    bsz, s, _ = h3.shape
    m = kvm.shape[1]
    cq = col_q // C_WIDTH
    return pl.pallas_call(
        _attn_c_kernel,
        grid=(bsz, s // tq),
        in_specs=[pl.BlockSpec((1, tq, C_WIDTH), lambda b, n: (b, n, cq)),
                  pl.BlockSpec((1, m, C_WIDTH), lambda b, n: (b, 0, 0)),
                  pl.BlockSpec((1, m, C_WIDTH), lambda b, n: (b, 0, 1))],
        out_specs=pl.BlockSpec((1, tq, C_WIDTH), lambda b, n: (b, n, 0)),
        out_shape=jax.ShapeDtypeStruct((bsz, s, C_WIDTH), BF16),
        compiler_params=_cparams(2, 32),
        name="attn_c",
    )(h3, kvm, kvm)


def _merge_kernel(alpha, *refs):
    group_refs, refs = refs[:4 * A_GROUPS], refs[4 * A_GROUPS:]
    (yb_ref, yc_ref, gate_ref, x_ref, bg_ref, wa_ref, wb_ref, wc_ref, wo_ref, g1_ref, b1_ref,
     wq_ref, x1_ref, x1b_ref, q_ref) = refs
    d = x_ref.shape[-1]
    halves = []
    for hf in range(2):
        o0, o1, o2 = (group_refs[4 * g + hf][...] for g in range(A_GROUPS))
        la, lb, lc = (group_refs[4 * g + 2 + hf][...] for g in range(A_GROUPS))
        m = jnp.maximum(jnp.maximum(la, lb), lc)
        wa, wb, wc = jnp.exp(la - m), jnp.exp(lb - m), jnp.exp(lc - m)
        inv = 1.0 / (wa + wb + wc)
        halves.append((wa * inv) * o0 + (wb * inv) * o1 + (wc * inv) * o2)
    ya = jnp.concatenate(halves, axis=1)

    def gate(n):
        pre = gate_ref[:, n * d:(n + 1) * d].astype(F32) + bg_ref[:, n * d:(n + 1) * d]
        return jax.nn.sigmoid(pre)

    merged = (gate(0) * _dot(ya.astype(BF16), wa_ref[...])
              + gate(1) * _dot(yb_ref[...], wb_ref[...])
              + gate(2) * _dot(yc_ref[...], wc_ref[...]))
    y = alpha * x_ref[...] + _dot(merged.astype(BF16), wo_ref[...])
    x1 = _layer_norm(y, g1_ref[...], b1_ref[...])
    x1_ref[...] = x1
    x1b = x1.astype(BF16)
    x1b_ref[...] = x1b
    q_ref[...] = _dot(x1b, wq_ref[...]).astype(q_ref.dtype)


def _merge(alpha, groups, yb, yc, h2, x2, b_gate, w_a, w_b, w_c, w_o, g1, b1, w_q, tm):
    t, d = x2.shape
    row = lambda w: pl.BlockSpec((tm, w), lambda i: (i, 0))
    full = lambda a: pl.BlockSpec(a.shape, lambda i: (0,) * a.ndim)
    weights = [b_gate, w_a, w_b, w_c, w_o, g1, b1, w_q]
    ins = [*groups, yb, yc, h2, x2, *weights]
    in_specs = ([row(A_GROUP_WIDTH // 2)] * len(groups)
                + [row(B_Q_WIDTH), row(C_WIDTH), row(N_GATES * d), row(d)]
                + [full(a) for a in weights])
    return pl.pallas_call(
        functools.partial(_merge_kernel, alpha),
        grid=(t // tm,),
        in_specs=in_specs,
        out_specs=[row(d), row(d), row(w_q.shape[1])],
        out_shape=[jax.ShapeDtypeStruct((t, d), F32), jax.ShapeDtypeStruct((t, d), BF16),
                   jax.ShapeDtypeStruct((t, w_q.shape[1]), BF16)],
        compiler_params=_cparams(1, 48),
        name="merge_ln1",
    )(*ins)


def _cmpx(a, b, need_lo=True):
    if b is None:
        return a, None
    if a is None:
        return b, None
    (va, ta), (vb, tb) = a, b
    a_first = (va > vb) | ((va == vb) & (ta < tb))
    hi = (jnp.maximum(va, vb), jnp.where(a_first, ta, tb))
    lo = (jnp.minimum(va, vb), jnp.where(a_first, tb, ta)) if need_lo else None
    return hi, lo


def _oddeven_sort_pairs(n):
    pairs = []

    def merge(lo, m, r):
        step = 2 * r
        if step < m:
            merge(lo, m, step)
            merge(lo + r, m, step)
            pairs.extend((i, i + r) for i in range(lo + r, lo + m - r, step))
        else:
            pairs.append((lo, lo + r))

    def sort(lo, m):
        if m > 1:
            sort(lo, m // 2)
            sort(lo + m // 2, m // 2)
            merge(lo, m, 1)

    sort(0, n)
    return pairs


def _sort_items(items):
    items = list(items)
    for i, j in _oddeven_sort_pairs(len(items)):
        items[i], items[j] = _cmpx(items[i], items[j])
    return items


def _bitonic_merge(seq):
    n = len(seq)
    if n == 1:
        return seq
    firsts, seconds = zip(*(_cmpx(seq[i], seq[i + n // 2]) for i in range(n // 2)))
    return _bitonic_merge(list(firsts)) + _bitonic_merge(list(seconds))


def _merge_top(a, b, k):
    p = 1
    while p < len(a) + len(b):
        p *= 2
    seq = list(a) + [None] * (p - len(a) - len(b)) + list(b)[::-1]
    while len(seq) > k:
        half = len(seq) // 2
        seq = [_cmpx(seq[i], seq[i + half], need_lo=False)[0] for i in range(half)]
    return [x for x in _bitonic_merge(seq) if x is not None][:k]


def _top_of_keys(items, k):
    runs = [_sort_items(items[i:i + k]) for i in range(0, len(items), k)]
    while len(runs) > 1:
        runs = [_merge_top(runs[i], runs[i + 1], k) for i in range(0, len(runs), 2)]
    return runs[0]


RETRIEVE_TOKENS = SUBLANES * LANES
SCORE_PITCH = N_KEYS + SUBLANES


def _tile_const(x):
    return jnp.full((SUBLANES, LANES), float(x), F32)


def _select_pairs(scores1, scores2):
    halves = [_top_of_keys([(s, _tile_const(k)) for k, s in enumerate(scores)], PEER_TOPK)
              for scores in (scores1, scores2)]
    (v1, i1), (v2, i2) = (tuple(zip(*half)) for half in halves)
    pair = lambda k1, k2: (v1[k1] + v2[k2], _tile_const(k1 * PEER_TOPK + k2))
    rows = [[pair(k1, k2) for k2 in range(PEER_TOPK // (k1 + 1))] for k1 in range(8)]
    column = [pair(k1, 0) for k1 in range(8, PEER_TOPK)]
    small = _merge_top(_merge_top(rows[5], rows[6], PEER_TOPK),
                       _merge_top(rows[7], rows[4], PEER_TOPK), PEER_TOPK)
    mid = _merge_top(_merge_top(rows[3], rows[2], PEER_TOPK), small, PEER_TOPK)
    best = _merge_top(rows[0], _merge_top(mid, _merge_top(rows[1], column, PEER_TOPK),
                                          PEER_TOPK), PEER_TOPK)
    top, pos = zip(*best)
    e = [jnp.exp(t - top[0]) for t in top]
    inv = 1.0 / functools.reduce(lambda a, b: a + b, e)
    out_i, out_j, out_g = [], [], []
    for k in range(PEER_TOPK):
        k1 = jnp.floor(pos[k] * (1.0 / PEER_TOPK))
        k2 = pos[k] - k1 * PEER_TOPK
        ei, ej = i1[0], i2[0]
        for r in range(1, PEER_TOPK):
            ei = jnp.where(k1 == r, i1[r], ei)
            ej = jnp.where(k2 == r, i2[r], ej)
        out_i.append(ei)
        out_j.append(ej)
        out_g.append(e[k] * inv)
    return out_i, out_j, out_g


def _retrieve_kernel(q_ref, keys_ref, i_ref, j_ref, g_ref, s_ref, ri_ref, rj_ref, rg_ref):
    def per_head(h, carry):
        col = pl.multiple_of(h * (2 * PEER_HALF), LANES)
        scores = []
        for c in range(2):
            keys = keys_ref[h, c]
            for ch in range(SUBLANES):
                qc = q_ref[ch * LANES:(ch + 1) * LANES, pl.ds(col, LANES)]
                s_ref[c, ch * SCORE_PITCH:ch * SCORE_PITCH + N_KEYS, :] = _dot_nt(
                    keys, qc[:, c * PEER_HALF:(c + 1) * PEER_HALF])
            scores.append([s_ref.at[c][pl.ds(k, SUBLANES, stride=SCORE_PITCH), :]
                           for k in range(N_KEYS)])
        out_i, out_j, out_g = _select_pairs(*scores)
        for k in range(PEER_TOPK):
            row0 = pl.multiple_of((h * PEER_TOPK + k) * SUBLANES, SUBLANES)
            ri_ref[pl.ds(row0, SUBLANES), :] = out_i[k]
            rj_ref[pl.ds(row0, SUBLANES), :] = out_j[k]
            rg_ref[pl.ds(row0, SUBLANES), :] = out_g[k]
        return carry

    lax.fori_loop(0, PEER_HEADS, per_head, 0)
    for ch in range(SUBLANES):
        rows = slice(ch * LANES, (ch + 1) * LANES)
        for r_ref, o_ref in ((ri_ref, i_ref), (rj_ref, j_ref), (rg_ref, g_ref)):
            o_ref[rows, :] = r_ref[pl.ds(ch, PEER_HEADS * PEER_TOPK, stride=SUBLANES), :].T


def _retrieve(q, keys):
    t, w = q.shape
    tt = RETRIEVE_TOKENS
    npair = PEER_HEADS * PEER_TOPK
    out = pl.BlockSpec((tt, npair), lambda i: (i, 0))
    slots = pltpu.VMEM((npair * SUBLANES, LANES), F32)
    return pl.pallas_call(
        _retrieve_kernel,
        grid=(t // tt,),
        in_specs=[pl.BlockSpec((tt, w), lambda i: (i, 0)),
                  pl.BlockSpec(keys.shape, lambda i: (0, 0, 0, 0))],
        out_specs=[out, out, out],
        out_shape=[jax.ShapeDtypeStruct((t, npair), F32)] * 3,
        scratch_shapes=[pltpu.VMEM((2, SUBLANES * SCORE_PITCH, LANES), F32), slots, slots, slots],
        compiler_params=_cparams(1, 32),
        name="peer_retrieve",
    )(q, keys)


GATE_TOKENS_PER_TRIP = 64


def _gate_matrix_kernel(i_ref, j_ref, g_ref, o_ref):
    tt, npair = i_ref.shape
    key = lax.broadcasted_iota(jnp.int32, (N_KEYS, npair), 0).astype(F32).astype(BF16)
    one = jnp.ones((N_KEYS, npair), BF16)
    zero = jnp.zeros((N_KEYS, npair), BF16)

    def per_trip(trip, carry):
        for u in range(GATE_TOKENS_PER_TRIP):
            t = trip * GATE_TOKENS_PER_TRIP + u
            irow = jnp.broadcast_to(i_ref[pl.ds(t, 1), :], (N_KEYS, npair)).astype(BF16)
            jrow = jnp.broadcast_to(j_ref[pl.ds(t, 1), :], (N_KEYS, npair)).astype(BF16)
            grow = jnp.broadcast_to(g_ref[pl.ds(t, 1), :], (N_KEYS, npair)).astype(BF16)
            a = jnp.where(key == irow, grow, zero)
            b = jnp.where(key == jrow, one, zero)
            grp, s = divmod(u, SUBLANES)
            start = (trip * (GATE_TOKENS_PER_TRIP // SUBLANES) + grp) * (SUBLANES * N_KEYS) + s
            o_ref[pl.ds(start, N_KEYS, stride=SUBLANES), :] = _dot_nt(a, b)
        return carry

    lax.fori_loop(0, tt // GATE_TOKENS_PER_TRIP, per_trip, 0)


def _gate_matrix(ei, ej, gate, tt):
    t, npair = ei.shape
    spec = pl.BlockSpec((tt, npair), lambda i: (i, 0))
    gm = pl.pallas_call(
        _gate_matrix_kernel,
        grid=(t // tt,),
        in_specs=[spec, spec, spec],
        out_specs=pl.BlockSpec((tt * N_KEYS, N_KEYS), lambda i: (i, 0)),
        out_shape=jax.ShapeDtypeStruct((t * N_KEYS, N_KEYS), F32),
        compiler_params=_cparams(1, 48),
        name="peer_gate_matrix",
    )(ei, ej, gate)
    return gm.reshape(t // SUBLANES, N_KEYS, SUBLANES, N_KEYS)


def _experts_kernel(alpha, sub, xb_ref, ut_ref, v_ref, gm_ref, x1_ref, g2_ref, b2_ref, o_ref, acc_ref):
    j = pl.program_id(1)

    @pl.when(j == 0)
    def _():
        acc_ref[...] = jnp.zeros_like(acc_ref)

    tt = xb_ref.shape[0]
    n_i = gm_ref.shape[1]
    for r0 in range(0, tt, sub):
        hid = _dot(xb_ref[r0:r0 + sub, :], ut_ref[...])
        act = 0.5 * hid * (1.0 + lax.erf(hid * (2.0 ** -0.5)))
        grp = slice(r0 // SUBLANES, (r0 + sub) // SUBLANES)
        gate = jnp.concatenate([gm_ref[grp, i].reshape(sub, N_KEYS) for i in range(n_i)], axis=1)
        p = (gate * act).astype(BF16)
        acc_ref[r0:r0 + sub, :] += _dot(p, v_ref[...])

    @pl.when(j == pl.num_programs(1) - 1)
    def _():
        y = alpha * x1_ref[...] + acc_ref[...]
        o_ref[...] = _layer_norm(y, g2_ref[...], b2_ref[...])


def _experts(alpha, x1b, ut, v, gm, x1, g2, b2, tt, ec, sub):
    t, d = x1.shape
    ne = ut.shape[1]
    return pl.pallas_call(
        functools.partial(_experts_kernel, alpha, sub),
        grid=(t // tt, ne // ec),
        in_specs=[pl.BlockSpec((tt, d), lambda i, j: (i, 0)),
                  pl.BlockSpec((d, ec), lambda i, j: (0, j)),
                  pl.BlockSpec((ec, d), lambda i, j: (j, 0)),
                  pl.BlockSpec((tt // SUBLANES, ec // N_KEYS, SUBLANES, N_KEYS),
                               lambda i, j: (i, j, 0, 0)),
                  pl.BlockSpec((tt, d), lambda i, j: (i, 0)),
                  pl.BlockSpec((1, d), lambda i, j: (0, 0)),
                  pl.BlockSpec((1, d), lambda i, j: (0, 0))],
        out_specs=pl.BlockSpec((tt, d), lambda i, j: (i, 0)),
        out_shape=jax.ShapeDtypeStruct((t, d), F32),
        scratch_shapes=[pltpu.VMEM((tt, d), F32)],
        compiler_params=_cparams(2, 56),
        name="peer_experts",
    )(x1b, ut, v, gm, x1, g2, b2)


def _layer(x, mem, rel_bias, w_in, b_gate, w_mem_kv, sinks, w_a, w_b, w_c, w_out, g1, b1,
           w_query, sub_keys, u_tab, v_tab, g2, b2, alpha):
    bsz, s, d = x.shape
    t = bsz * s

    def a_cols(g):
        return [w_in[:, part * A_WIDTH + g * A_GROUP_WIDTH:part * A_WIDTH + (g + 1) * A_GROUP_WIDTH]
                for part in range(3)]

    rest_at = 3 * A_WIDTH
    gates_at = rest_at + B_Q_WIDTH + 2 * B_KV_WIDTH + C_WIDTH
    w_nat = jnp.concatenate([w_in[:, gates_at:], *a_cols(0), w_in[:, rest_at:gates_at]],
                            axis=1).astype(BF16)
    dils = [dil for _, dil in A_PAIRS[1:]]
    w_dil = [jnp.concatenate(a_cols(g), axis=1).astype(BF16) for g in range(1, A_GROUPS)]
    row_w = w_nat.shape[1]
    col_aq = N_GATES * d
    col_bq = col_aq + 3 * A_GROUP_WIDTH
    col_bk = col_bq + B_Q_WIDTH
    col_bv = col_bk + B_KV_WIDTH
    col_cq = col_bv + B_KV_WIDTH

    x2 = x.reshape(t, d)
    h2, *h_dil = _in_proj(x2, bsz, w_nat, w_dil, dils, 256)
    h3 = h2.reshape(bsz, s, row_w)

    bias = _bias_tiles(rel_bias)
    groups = []
    for g in range(A_GROUPS):
        if g == 0:
            hr, c0 = h3.reshape(bsz, 1, s, row_w), col_aq // A_GROUP_WIDTH
        else:
            hr, c0 = h_dil[g - 1], 0
        groups += _attn_a_group(hr, bias, g, c0, c0 + 1, c0 + 2)

    yb = _attn_b(h3, bias, sinks, col_bq, col_bk, col_bv).reshape(t, B_Q_WIDTH)

    m = mem.shape[1]
    kvm = _project(mem.reshape(bsz * m, d), w_mem_kv.astype(BF16), 256, 2 * C_WIDTH, "mem_kv")
    yc = _attn_c(h3, kvm.reshape(bsz, m, 2 * C_WIDTH), col_cq, 512).reshape(t, C_WIDTH)

    x1, x1b, q = _merge(alpha, groups, yb, yc, h2, x2, b_gate.reshape(1, -1).astype(F32),
                        w_a.astype(BF16), w_b.astype(BF16), w_c.astype(BF16), w_out.astype(BF16),
                        g1.reshape(1, d).astype(F32), b1.reshape(1, d).astype(F32),
                        w_query.astype(BF16), 256)

    ei, ej, gate = _retrieve(q, sub_keys.astype(BF16))
    gm = _gate_matrix(ei, ej, gate, 256)
    out = _experts(alpha, x1b, u_tab.astype(BF16).T, v_tab.astype(BF16), gm, x1,
                   g2.reshape(1, d).astype(F32), b2.reshape(1, d).astype(F32), 1024, 1024, 256)
    return out.reshape(bsz, s, d)


def kernel(x, mem, rel_bias, w_in, b_gate, w_mem_kv, sinks, w_branch_a, w_branch_b, w_branch_c,
           w_out, ln1_g, ln1_b, peer_w_query, peer_sub_keys, peer_u, peer_v, ln2_g, ln2_b):
    depth = w_in.shape[0]
    alpha = (2.0 * depth) ** 0.25
    for l in range(depth):
        x = _layer(x, mem, rel_bias, w_in[l], b_gate[l], w_mem_kv[l], sinks[l], w_branch_a[l],
                   w_branch_b[l], w_branch_c[l], w_out[l], ln1_g[l], ln1_b[l], peer_w_query[l],
                   peer_sub_keys[l], peer_u[l], peer_v[l], ln2_g[l], ln2_b[l], alpha)
    return x
```

```python
import functools
import math

import numpy as np
import jax
import jax.numpy as jnp
from jax import lax
from jax.experimental import pallas as pl
from jax.experimental.pallas import tpu as pltpu

F32 = jnp.float32
BF16 = jnp.bfloat16

HEAD_DIM = 64
BLOCK = 128
A_PAIRS = ((128, 1), (512, 4), (2048, 16))
A_GROUPS = 3
A_HEADS = 4
A_GROUP_WIDTH = A_HEADS * HEAD_DIM
A_WIDTH = A_GROUPS * A_GROUP_WIDTH
B_Q_HEADS = 8
B_KV_HEADS = 2
B_WINDOW = 128
B_Q_WIDTH = B_Q_HEADS * HEAD_DIM
B_KV_WIDTH = B_KV_HEADS * HEAD_DIM
C_HEADS = 4
C_HEAD_DIM = 128
C_WIDTH = C_HEADS * C_HEAD_DIM
N_GATES = 3
N_BUCKETS = 32
MAX_DISTANCE = 2048
PEER_HEADS = 8
N_KEYS = 128
PEER_TOPK = 16
PEER_HALF = 64
LN_EPS = 1e-5
NEG = -1e30

V7X_VMEM_BYTES = 64 * 1024 * 1024
SUBLANES = 8
LANES = 128


def _cparams(n_grid, vmem_mb):
    return pltpu.CompilerParams(
        dimension_semantics=("arbitrary",) * n_grid,
        vmem_limit_bytes=vmem_mb * 1024 * 1024,
    )


def _dot(a, b):
    return jnp.dot(a, b, preferred_element_type=F32)


def _dot_nt(a, b):
    return lax.dot_general(a, b, (((1,), (1,)), ((), ())), preferred_element_type=F32)


def _layer_norm(y, g, b):
    mu = jnp.mean(y, axis=-1, keepdims=True)
    yc = y - mu
    var = jnp.mean(yc * yc, axis=-1, keepdims=True)
    return yc * lax.rsqrt(var + LN_EPS) * g + b


def _proj_kernel(x_ref, w_ref, o_ref):
    o_ref[...] = _dot(x_ref[...].astype(BF16), w_ref[...]).astype(o_ref.dtype)


def _project(x, w, tm, tn, name):
    m, k = x.shape
    n = w.shape[1]
    return pl.pallas_call(
        _proj_kernel,
        grid=(n // tn, m // tm),
        in_specs=[pl.BlockSpec((tm, k), lambda j, i: (i, 0)),
                  pl.BlockSpec((k, tn), lambda j, i: (0, j))],
        out_specs=pl.BlockSpec((tm, tn), lambda j, i: (i, j)),
        out_shape=jax.ShapeDtypeStruct((m, n), BF16),
        compiler_params=_cparams(2, 48),
        name=name,
    )(x, w)


def _in_proj_kernel(x_ref, wn_ref, w1_ref, w2_ref, p1_ref, p2_ref, hn_ref, h1_ref, h2_ref):
    tm = x_ref.shape[0]
    xb = x_ref[...].astype(BF16)
    hn_ref[...] = _dot(xb, wn_ref[...]).astype(hn_ref.dtype)
    for w_ref, p_ref, h_ref in ((w1_ref, p1_ref, h1_ref), (w2_ref, p2_ref, h2_ref)):
        r = h_ref.shape[1]
        h = _dot(xb, w_ref[...]).astype(BF16)
        hp = _dot(p_ref[...], h).astype(h_ref.dtype)
        h_ref[0] = hp.reshape(r, tm // r, hp.shape[-1])


def _class_major_permutation(tm, r):
    dst = np.arange(tm)
    src = (dst % (tm // r)) * r + dst // (tm // r)
    p = np.zeros((tm, tm), np.float32)
    p[dst, src] = 1.0
    return jnp.asarray(p, BF16)


def _in_proj(x2, bsz, w_nat, w_dil, dils, tm):
    t, d = x2.shape
    s = t // bsz
    nt = s // tm
    full = lambda a: pl.BlockSpec(a.shape, lambda b, i: (0, 0))
    dil_spec = lambda r, w: pl.BlockSpec((1, r, tm // r, w), lambda b, i: (b, 0, i, 0))
    perms = [_class_major_permutation(tm, r) for r in dils]
    return pl.pallas_call(
        _in_proj_kernel,
        grid=(bsz, nt),
        in_specs=[pl.BlockSpec((tm, d), lambda b, i: (b * nt + i, 0)),
                  full(w_nat), full(w_dil[0]), full(w_dil[1]), full(perms[0]), full(perms[1])],
        out_specs=[pl.BlockSpec((tm, w_nat.shape[1]), lambda b, i: (b * nt + i, 0)),
                   dil_spec(dils[0], w_dil[0].shape[1]), dil_spec(dils[1], w_dil[1].shape[1])],
        out_shape=[jax.ShapeDtypeStruct((t, w_nat.shape[1]), BF16),
                   jax.ShapeDtypeStruct((bsz, dils[0], s // dils[0], w_dil[0].shape[1]), BF16),
                   jax.ShapeDtypeStruct((bsz, dils[1], s // dils[1], w_dil[1].shape[1]), BF16)],
        compiler_params=_cparams(2, 56),
        name="in_proj",
    )(x2, w_nat, *w_dil, *perms)


def _t5_bucket(dist):
    n = np.asarray(dist, dtype=np.int32)
    max_exact = N_BUCKETS // 2
    nf = np.maximum(n, 1).astype(np.float32)
    scale = np.float32(math.log(MAX_DISTANCE / max_exact))
    large = max_exact + (np.log(nf / np.float32(max_exact)) / scale
                         * np.float32(N_BUCKETS - max_exact)).astype(np.int32)
    large = np.minimum(large, N_BUCKETS - 1)
    return np.where(n < max_exact, n, large).astype(np.int32)


def _bucket_tiles(dist_scale, max_off):
    i = np.arange(BLOCK)[:, None]
    j = np.arange(2 * BLOCK)[None, :]
    off = BLOCK + i - j
    bucket = _t5_bucket(np.clip(off, 0, max_off) * dist_scale)
    valid = (off >= 0) & (off <= max_off)
    general = np.where(valid, bucket, -1)
    first = np.where(valid & (j >= BLOCK), bucket, -1)
    return np.stack([first, general], axis=0).astype(np.int32)


def _bias_kernel(table_ref, bucket_ref, o_ref):
    h = pl.program_id(0)
    col = jnp.where(h < B_Q_HEADS, h + A_GROUPS * A_HEADS, h - B_Q_HEADS)
    for var in range(2):
        bk = bucket_ref[0, var]
        acc = jnp.full(bk.shape, NEG, F32)
        for b in range(N_BUCKETS):
            acc = jnp.where(bk == b, table_ref[b, col], acc)
        o_ref[var, 0] = acc


def _bias_tiles(rel_bias):
    kinds = [_bucket_tiles(1, B_WINDOW - 1)] + [_bucket_tiles(d, w // d) for w, d in A_PAIRS]
    buckets = jnp.asarray(np.stack(kinds, axis=0))
    n_heads = B_Q_HEADS + A_GROUPS * A_HEADS

    def kind(h):
        return jnp.where(h < B_Q_HEADS, 0, 1 + (h - B_Q_HEADS) // A_HEADS)

    return pl.pallas_call(
        _bias_kernel,
        grid=(n_heads,),
        in_specs=[pl.BlockSpec(memory_space=pltpu.SMEM),
                  pl.BlockSpec((1, 2, BLOCK, 2 * BLOCK), lambda h: (kind(h), 0, 0, 0))],
        out_specs=pl.BlockSpec((2, 1, BLOCK, 2 * BLOCK), lambda h: (0, h, 0, 0)),
        out_shape=jax.ShapeDtypeStruct((2, n_heads, BLOCK, 2 * BLOCK), F32),
        compiler_params=_cparams(1, 32),
        name="bias_tiles",
    )(rel_bias.astype(F32), buckets)


def _attn_a_kernel(r, q_ref, kp_ref, kc_ref, vp_ref, vc_ref, bias_ref, o0_ref, o1_ref, l0_ref, l1_ref):
    c = pl.program_id(2)
    q = q_ref[0, 0]
    kcat = jnp.concatenate([kp_ref[0, 0], kc_ref[0, 0]], axis=0)
    vcat = jnp.concatenate([vp_ref[0, 0], vc_ref[0, 0]], axis=0)
    scale = HEAD_DIM ** -0.5
    rows = pl.ds(c, BLOCK, stride=r) if r > 1 else slice(None)
    heads = [slice(HEAD_DIM * h, HEAD_DIM * (h + 1)) for h in range(A_HEADS)]
    logits = jnp.concatenate([_dot_nt(q[:, sl], kcat[:, sl]) for sl in heads], axis=0)
    logits = logits * scale + bias_ref[0].reshape(A_HEADS * BLOCK, 2 * BLOCK)
    m = jnp.max(logits, axis=-1, keepdims=True)
    e = jnp.exp(logits - m)
    s = jnp.sum(e, axis=-1, keepdims=True)
    p = (e * (1.0 / s)).astype(BF16)
    lse = m + jnp.log(s)
    outs = [_dot(p[BLOCK * h:BLOCK * (h + 1)], vcat[:, sl]) for h, sl in enumerate(heads)]
    lses = [jnp.broadcast_to(lse[BLOCK * h:BLOCK * (h + 1)], (BLOCK, HEAD_DIM))
            for h in range(A_HEADS)]
    o0_ref[rows, :] = jnp.concatenate(outs[:2], axis=1)
    o1_ref[rows, :] = jnp.concatenate(outs[2:], axis=1)
    l0_ref[rows, :] = jnp.concatenate(lses[:2], axis=1)
    l1_ref[rows, :] = jnp.concatenate(lses[2:], axis=1)


def _attn_a_group(hr, bias, g, col_q, col_k, col_v):
    bsz, r, ln, _ = hr.shape
    nblk = ln // BLOCK
    half = A_GROUP_WIDTH // 2

    def spec(col, prev):
        if prev:
            return pl.BlockSpec((1, 1, BLOCK, A_GROUP_WIDTH),
                                lambda b, n, c: (b, c, jnp.maximum(n - 1, 0), col))
        return pl.BlockSpec((1, 1, BLOCK, A_GROUP_WIDTH), lambda b, n, c: (b, c, n, col))

    out_spec = pl.BlockSpec((BLOCK * r, half), lambda b, n, c: (b * nblk + n, 0))
    return pl.pallas_call(
        functools.partial(_attn_a_kernel, r),
        grid=(bsz, nblk, r),
        in_specs=[spec(col_q, False), spec(col_k, True), spec(col_k, False),
                  spec(col_v, True), spec(col_v, False),
                  pl.BlockSpec((1, A_HEADS, BLOCK, 2 * BLOCK),
                               lambda b, n, c: (jnp.minimum(n, 1), B_Q_HEADS // A_HEADS + g, 0, 0))],
        out_specs=[out_spec] * 4,
        out_shape=[jax.ShapeDtypeStruct((bsz * ln * r, half), F32)] * 4,
        compiler_params=_cparams(3, 32),
        name=f"attn_a{g}",
    )(hr, hr, hr, hr, hr, bias)


def _attn_b_kernel(sinks_ref, q0_ref, q1_ref, kp_ref, kc_ref, vp_ref, vc_ref, bias_ref, o_ref):
    kcat = jnp.concatenate([kp_ref[0], kc_ref[0]], axis=0)
    vcat = jnp.concatenate([vp_ref[0], vc_ref[0]], axis=0)
    scale = HEAD_DIM ** -0.5
    per_kv = B_Q_HEADS // B_KV_HEADS
    parts, sink_rows = [], []
    for h in range(B_Q_HEADS):
        kv = h // per_kv
        q = (q0_ref if kv == 0 else q1_ref)[0]
        qsl = slice(HEAD_DIM * (h % per_kv), HEAD_DIM * (h % per_kv + 1))
        parts.append(_dot_nt(q[:, qsl], kcat[:, HEAD_DIM * kv:HEAD_DIM * (kv + 1)]))
        sink_rows.append(jnp.full((BLOCK, 1), sinks_ref[h], F32))
    logits = jnp.concatenate(parts, axis=0) * scale + bias_ref[0].reshape(B_Q_HEADS * BLOCK, 2 * BLOCK)
    sink = jnp.concatenate(sink_rows, axis=0)
    m = jnp.maximum(jnp.max(logits, axis=-1, keepdims=True), sink)
    e = jnp.exp(logits - m)
    denom = jnp.sum(e, axis=-1, keepdims=True) + jnp.exp(sink - m)
    p = (e * (1.0 / denom)).astype(BF16)
    outs = [_dot(p[BLOCK * h:BLOCK * (h + 1)],
                 vcat[:, HEAD_DIM * (h // per_kv):HEAD_DIM * (h // per_kv + 1)])
            for h in range(B_Q_HEADS)]
    o_ref[0] = jnp.concatenate(outs, axis=1).astype(o_ref.dtype)


def _attn_b(h3, bias, sinks, col_q, col_k, col_v):
    bsz, s, _ = h3.shape
    nb = s // BLOCK
    qw = B_Q_WIDTH // B_KV_HEADS
    cq = col_q // qw
    ck = col_k // B_KV_WIDTH
    cv = col_v // B_KV_WIDTH

    def kv_spec(col, prev):
        if prev:
            return pl.BlockSpec((1, BLOCK, B_KV_WIDTH), lambda b, n: (b, jnp.maximum(n - 1, 0), col))
        return pl.BlockSpec((1, BLOCK, B_KV_WIDTH), lambda b, n: (b, n, col))

    return pl.pallas_call(
        _attn_b_kernel,
        grid=(bsz, nb),
        in_specs=[pl.BlockSpec(memory_space=pltpu.SMEM),
                  pl.BlockSpec((1, BLOCK, qw), lambda b, n: (b, n, cq)),
                  pl.BlockSpec((1, BLOCK, qw), lambda b, n: (b, n, cq + 1)),
                  kv_spec(ck, True), kv_spec(ck, False), kv_spec(cv, True), kv_spec(cv, False),
                  pl.BlockSpec((1, B_Q_HEADS, BLOCK, 2 * BLOCK),
                               lambda b, n: (jnp.minimum(n, 1), 0, 0, 0))],
        out_specs=pl.BlockSpec((1, BLOCK, B_Q_WIDTH), lambda b, n: (b, n, 0)),
        out_shape=jax.ShapeDtypeStruct((bsz, s, B_Q_WIDTH), BF16),
        compiler_params=_cparams(2, 32),
        name="attn_b",
    )(sinks.astype(F32), h3, h3, h3, h3, h3, h3, bias)


def _attn_c_kernel(q_ref, k_ref, v_ref, o_ref):
    q = q_ref[0]
    k = k_ref[0]
    v = v_ref[0]
    scale = C_HEAD_DIM ** -0.5
    for h in range(C_HEADS):
        sl = slice(C_HEAD_DIM * h, C_HEAD_DIM * (h + 1))
        logits = _dot_nt(q[:, sl], k[:, sl]) * scale
        m = jnp.max(logits, axis=-1, keepdims=True)
        e = jnp.exp(logits - m)
        inv = 1.0 / jnp.sum(e, axis=-1, keepdims=True)
        o_ref[0, :, sl] = _dot((e * inv).astype(BF16), v[:, sl]).astype(o_ref.dtype)


def _attn_c(h3, kvm, col_q, tq):
    bsz, s, _ = h3.shape
    m = kvm.shape[1]
    cq = col_q // C_WIDTH
    return pl.pallas_call(
        _attn_c_kernel,
        grid=(bsz, s // tq),
        in_specs=[pl.BlockSpec((1, tq, C_WIDTH), lambda b, n: (b, n, cq)),
                  pl.BlockSpec((1, m, C_WIDTH), lambda b, n: (b, 0, 0)),
                  pl.BlockSpec((1, m, C_WIDTH), lambda b, n: (b, 0, 1))],
        out_specs=pl.BlockSpec((1, tq, C_WIDTH), lambda b, n: (b, n, 0)),
        out_shape=jax.ShapeDtypeStruct((bsz, s, C_WIDTH), BF16),
        compiler_params=_cparams(2, 32),
        name="attn_c",
    )(h3, kvm, kvm)


def _merge_kernel(alpha, *refs):
    group_refs, refs = refs[:4 * A_GROUPS], refs[4 * A_GROUPS:]
    (yb_ref, yc_ref, gate_ref, x_ref, bg_ref, wa_ref, wb_ref, wc_ref, wo_ref, g1_ref, b1_ref,
     wq_ref, x1_ref, x1b_ref, q_ref) = refs
    d = x_ref.shape[-1]
    halves = []
    for hf in range(2):
        o0, o1, o2 = (group_refs[4 * g + hf][...] for g in range(A_GROUPS))
        la, lb, lc = (group_refs[4 * g + 2 + hf][...] for g in range(A_GROUPS))
        m = jnp.maximum(jnp.maximum(la, lb), lc)
        wa, wb, wc = jnp.exp(la - m), jnp.exp(lb - m), jnp.exp(lc - m)
        inv = 1.0 / (wa + wb + wc)
        halves.append((wa * inv) * o0 + (wb * inv) * o1 + (wc * inv) * o2)
    ya = jnp.concatenate(halves, axis=1)

    def gate(n):
        pre = gate_ref[:, n * d:(n + 1) * d].astype(F32) + bg_ref[:, n * d:(n + 1) * d]
        return jax.nn.sigmoid(pre)

    merged = (gate(0) * _dot(ya.astype(BF16), wa_ref[...])
              + gate(1) * _dot(yb_ref[...], wb_ref[...])
              + gate(2) * _dot(yc_ref[...], wc_ref[...]))
    y = alpha * x_ref[...] + _dot(merged.astype(BF16), wo_ref[...])
    x1 = _layer_norm(y, g1_ref[...], b1_ref[...])
    x1_ref[...] = x1
    x1b = x1.astype(BF16)
    x1b_ref[...] = x1b
    q_ref[...] = _dot(x1b, wq_ref[...]).astype(q_ref.dtype)


def _merge(alpha, groups, yb, yc, h2, x2, b_gate, w_a, w_b, w_c, w_o, g1, b1, w_q, tm):
    t, d = x2.shape
    row = lambda w: pl.BlockSpec((tm, w), lambda i: (i, 0))
    full = lambda a: pl.BlockSpec(a.shape, lambda i: (0,) * a.ndim)
    weights = [b_gate, w_a, w_b, w_c, w_o, g1, b1, w_q]
    ins = [*groups, yb, yc, h2, x2, *weights]
    in_specs = ([row(A_GROUP_WIDTH // 2)] * len(groups)
                + [row(B_Q_WIDTH), row(C_WIDTH), row(N_GATES * d), row(d)]
                + [full(a) for a in weights])
    return pl.pallas_call(
        functools.partial(_merge_kernel, alpha),
        grid=(t // tm,),
        in_specs=in_specs,
        out_specs=[row(d), row(d), row(w_q.shape[1])],
        out_shape=[jax.ShapeDtypeStruct((t, d), F32), jax.ShapeDtypeStruct((t, d), BF16),
                   jax.ShapeDtypeStruct((t, w_q.shape[1]), BF16)],
        compiler_params=_cparams(1, 48),
        name="merge_ln1",
    )(*ins)


def _cmpx(a, b, need_lo=True):
    if b is None:
        return a, None
    if a is None:
        return b, None
    (va, ta), (vb, tb) = a, b
    a_first = (va > vb) | ((va == vb) & (ta < tb))
    hi = (jnp.maximum(va, vb), jnp.where(a_first, ta, tb))
    lo = (jnp.minimum(va, vb), jnp.where(a_first, tb, ta)) if need_lo else None
    return hi, lo


def _oddeven_sort_pairs(n):
    pairs = []

    def merge(lo, m, r):
        step = 2 * r
        if step < m:
            merge(lo, m, step)
            merge(lo + r, m, step)
            pairs.extend((i, i + r) for i in range(lo + r, lo + m - r, step))
        else:
            pairs.append((lo, lo + r))

    def sort(lo, m):
        if m > 1:
            sort(lo, m // 2)
            sort(lo + m // 2, m // 2)
            merge(lo, m, 1)

    sort(0, n)
    return pairs


def _sort_items(items):
    items = list(items)
    for i, j in _oddeven_sort_pairs(len(items)):
        items[i], items[j] = _cmpx(items[i], items[j])
    return items


def _bitonic_merge(seq):
    n = len(seq)
    if n == 1:
        return seq
    firsts, seconds = zip(*(_cmpx(seq[i], seq[i + n // 2]) for i in range(n // 2)))
    return _bitonic_merge(list(firsts)) + _bitonic_merge(list(seconds))


def _merge_top(a, b, k):
    p = 1
    while p < len(a) + len(b):
        p *= 2
    seq = list(a) + [None] * (p - len(a) - len(b)) + list(b)[::-1]
    while len(seq) > k:
        half = len(seq) // 2
        seq = [_cmpx(seq[i], seq[i + half], need_lo=False)[0] for i in range(half)]
    return [x for x in _bitonic_merge(seq) if x is not None][:k]


def _top_of_keys(items, k):
    runs = [_sort_items(items[i:i + k]) for i in range(0, len(items), k)]
    while len(runs) > 1:
        runs = [_merge_top(runs[i], runs[i + 1], k) for i in range(0, len(runs), 2)]
    return runs[0]


RETRIEVE_TOKENS = SUBLANES * LANES
SCORE_PITCH = N_KEYS + SUBLANES


def _tile_const(x):
    return jnp.full((SUBLANES, LANES), float(x), F32)


def _select_pairs(scores1, scores2):
    halves = [_top_of_keys([(s, _tile_const(k)) for k, s in enumerate(scores)], PEER_TOPK)
              for scores in (scores1, scores2)]
    (v1, i1), (v2, i2) = (tuple(zip(*half)) for half in halves)
    pair = lambda k1, k2: (v1[k1] + v2[k2], _tile_const(k1 * PEER_TOPK + k2))
    rows = [[pair(k1, k2) for k2 in range(PEER_TOPK // (k1 + 1))] for k1 in range(8)]
    column = [pair(k1, 0) for k1 in range(8, PEER_TOPK)]
    small = _merge_top(_merge_top(rows[5], rows[6], PEER_TOPK),
                       _merge_top(rows[7], rows[4], PEER_TOPK), PEER_TOPK)
    mid = _merge_top(_merge_top(rows[3], rows[2], PEER_TOPK), small, PEER_TOPK)
    best = _merge_top(rows[0], _merge_top(mid, _merge_top(rows[1], column, PEER_TOPK),
                                          PEER_TOPK), PEER_TOPK)
    top, pos = zip(*best)
    e = [jnp.exp(t - top[0]) for t in top]
    inv = 1.0 / functools.reduce(lambda a, b: a + b, e)
    out_i, out_j, out_g = [], [], []
    for k in range(PEER_TOPK):
        k1 = jnp.floor(pos[k] * (1.0 / PEER_TOPK))
        k2 = pos[k] - k1 * PEER_TOPK
        ei, ej = i1[0], i2[0]
        for r in range(1, PEER_TOPK):
            ei = jnp.where(k1 == r, i1[r], ei)
            ej = jnp.where(k2 == r, i2[r], ej)
        out_i.append(ei)
        out_j.append(ej)
        out_g.append(e[k] * inv)
    return out_i, out_j, out_g


def _retrieve_kernel(q_ref, keys_ref, i_ref, j_ref, g_ref, s_ref, ri_ref, rj_ref, rg_ref):
    def per_head(h, carry):
        col = pl.multiple_of(h * (2 * PEER_HALF), LANES)
        scores = []
        for c in range(2):
            keys = keys_ref[h, c]
            for ch in range(SUBLANES):
                qc = q_ref[ch * LANES:(ch + 1) * LANES, pl.ds(col, LANES)]
                s_ref[c, ch * SCORE_PITCH:ch * SCORE_PITCH + N_KEYS, :] = _dot_nt(
                    keys, qc[:, c * PEER_HALF:(c + 1) * PEER_HALF])
            scores.append([s_ref.at[c][pl.ds(k, SUBLANES, stride=SCORE_PITCH), :]
                           for k in range(N_KEYS)])
        out_i, out_j, out_g = _select_pairs(*scores)
        for k in range(PEER_TOPK):
            row0 = pl.multiple_of((h * PEER_TOPK + k) * SUBLANES, SUBLANES)
            ri_ref[pl.ds(row0, SUBLANES), :] = out_i[k]
            rj_ref[pl.ds(row0, SUBLANES), :] = out_j[k]
            rg_ref[pl.ds(row0, SUBLANES), :] = out_g[k]
        return carry

    lax.fori_loop(0, PEER_HEADS, per_head, 0)
    for ch in range(SUBLANES):
        rows = slice(ch * LANES, (ch + 1) * LANES)
        for r_ref, o_ref in ((ri_ref, i_ref), (rj_ref, j_ref), (rg_ref, g_ref)):
            o_ref[rows, :] = r_ref[pl.ds(ch, PEER_HEADS * PEER_TOPK, stride=SUBLANES), :].T


def _retrieve(q, keys):
    t, w = q.shape
    tt = RETRIEVE_TOKENS
    npair = PEER_HEADS * PEER_TOPK
    out = pl.BlockSpec((tt, npair), lambda i: (i, 0))
    slots = pltpu.VMEM((npair * SUBLANES, LANES), F32)
    return pl.pallas_call(
        _retrieve_kernel,
        grid=(t // tt,),
        in_specs=[pl.BlockSpec((tt, w), lambda i: (i, 0)),
                  pl.BlockSpec(keys.shape, lambda i: (0, 0, 0, 0))],
        out_specs=[out, out, out],
        out_shape=[jax.ShapeDtypeStruct((t, npair), F32)] * 3,
        scratch_shapes=[pltpu.VMEM((2, SUBLANES * SCORE_PITCH, LANES), F32), slots, slots, slots],
        compiler_params=_cparams(1, 32),
        name="peer_retrieve",
    )(q, keys)


GATE_TOKENS_PER_TRIP = 64
GATE_WORD_ROWS = N_KEYS // 2
BF16_HALF_ULP = 0x8000
BF16_HIGH_MASK = -0x10000


def _gate_matrix_kernel(i_ref, j_ref, g_ref, o_ref):
    tt, npair = i_ref.shape
    key = lax.broadcasted_iota(jnp.int32, (N_KEYS, npair), 0).astype(F32).astype(BF16)
    one = jnp.ones((N_KEYS, npair), BF16)
    zero = jnp.zeros((N_KEYS, npair), BF16)

    def per_trip(trip, carry):
        for u in range(GATE_TOKENS_PER_TRIP):
            t = trip * GATE_TOKENS_PER_TRIP + u
            irow = jnp.broadcast_to(i_ref[pl.ds(t, 1), :], (N_KEYS, npair)).astype(BF16)
            jrow = jnp.broadcast_to(j_ref[pl.ds(t, 1), :], (N_KEYS, npair)).astype(BF16)
            grow = jnp.broadcast_to(g_ref[pl.ds(t, 1), :], (N_KEYS, npair)).astype(BF16)
            a = jnp.where(key == irow, grow, zero)
            b = jnp.where(key == jrow, one, zero)
            tile = _dot_nt(a, b)
            bits = lax.bitcast_convert_type(tile, jnp.int32) + BF16_HALF_ULP
            words = (jnp.right_shift(bits[:GATE_WORD_ROWS], 16)
                     | (bits[GATE_WORD_ROWS:] & BF16_HIGH_MASK))
            grp, s = divmod(u, SUBLANES)
            start = ((trip * (GATE_TOKENS_PER_TRIP // SUBLANES) + grp)
                     * (SUBLANES * GATE_WORD_ROWS) + s)
            o_ref[pl.ds(start, GATE_WORD_ROWS, stride=SUBLANES), :] = words
        return carry

    lax.fori_loop(0, tt // GATE_TOKENS_PER_TRIP, per_trip, 0)


def _gate_matrix(ei, ej, gate, tt):
    t, npair = ei.shape
    spec = pl.BlockSpec((tt, npair), lambda i: (i, 0))
    gm = pl.pallas_call(
        _gate_matrix_kernel,
        grid=(t // tt,),
        in_specs=[spec, spec, spec],
        out_specs=pl.BlockSpec((tt * GATE_WORD_ROWS, N_KEYS), lambda i: (i, 0)),
        out_shape=jax.ShapeDtypeStruct((t * GATE_WORD_ROWS, N_KEYS), jnp.int32),
        compiler_params=_cparams(1, 48),
        name="peer_gate_matrix",
    )(ei, ej, gate)
    return gm.reshape(t // SUBLANES, GATE_WORD_ROWS, SUBLANES, N_KEYS)


def _experts_kernel(alpha, sub, xb_ref, ut_ref, v_ref, gm_ref, x1_ref, g2_ref, b2_ref, o_ref, acc_ref):
    j = pl.program_id(1)

    @pl.when(j == 0)
    def _():
        acc_ref[...] = jnp.zeros_like(acc_ref)

    tt = xb_ref.shape[0]
    n_rows = gm_ref.shape[1]
    for r0 in range(0, tt, sub):
        hid = _dot(xb_ref[r0:r0 + sub, :], ut_ref[...])
        act = 0.5 * hid * (1.0 + lax.erf(hid * (2.0 ** -0.5)))
        grp = slice(r0 // SUBLANES, (r0 + sub) // SUBLANES)
        words = [gm_ref[grp, i].reshape(sub, N_KEYS) for i in range(n_rows)]
        gate = jnp.concatenate(
            [lax.bitcast_convert_type(jnp.left_shift(w, 16), F32) for w in words]
            + [lax.bitcast_convert_type(w & BF16_HIGH_MASK, F32) for w in words], axis=1)
        acc_ref[r0:r0 + sub, :] += _dot((gate * act).astype(BF16), v_ref[...])

    @pl.when(j == pl.num_programs(1) - 1)
    def _():
        y = alpha * x1_ref[...] + acc_ref[...]
        o_ref[...] = _layer_norm(y, g2_ref[...], b2_ref[...])


def _experts(alpha, x1b, ut, v, gm, x1, g2, b2, tt, ec, sub):
    t, d = x1.shape
    ne = ut.shape[1]
    return pl.pallas_call(
        functools.partial(_experts_kernel, alpha, sub),
        grid=(t // tt, ne // ec),
        in_specs=[pl.BlockSpec((tt, d), lambda i, j: (i, 0)),
                  pl.BlockSpec((d, ec), lambda i, j: (0, j)),
                  pl.BlockSpec((ec, d), lambda i, j: (j, 0)),
                  pl.BlockSpec((tt // SUBLANES, ec // (2 * N_KEYS), SUBLANES, N_KEYS),
                               lambda i, j: (i, j, 0, 0)),
                  pl.BlockSpec((tt, d), lambda i, j: (i, 0)),
                  pl.BlockSpec((1, d), lambda i, j: (0, 0)),
                  pl.BlockSpec((1, d), lambda i, j: (0, 0))],
        out_specs=pl.BlockSpec((tt, d), lambda i, j: (i, 0)),
        out_shape=jax.ShapeDtypeStruct((t, d), F32),
        scratch_shapes=[pltpu.VMEM((tt, d), F32)],
        compiler_params=_cparams(2, 56),
        name="peer_experts",
    )(x1b, ut, v, gm, x1, g2, b2)


def _chunk_order(table, ec):
    ne, d = table.shape
    half = ec // 2
    return table.reshape(2, ne // ec, half, d).transpose(1, 0, 2, 3).reshape(ne, d)


def _layer(x, mem, rel_bias, w_in, b_gate, w_mem_kv, sinks, w_a, w_b, w_c, w_out, g1, b1,
           w_query, sub_keys, u_tab, v_tab, g2, b2, alpha):
    bsz, s, d = x.shape
    t = bsz * s

    def a_cols(g):
        return [w_in[:, part * A_WIDTH + g * A_GROUP_WIDTH:part * A_WIDTH + (g + 1) * A_GROUP_WIDTH]
                for part in range(3)]

    rest_at = 3 * A_WIDTH
    gates_at = rest_at + B_Q_WIDTH + 2 * B_KV_WIDTH + C_WIDTH
    w_nat = jnp.concatenate([w_in[:, gates_at:], *a_cols(0), w_in[:, rest_at:gates_at]],
                            axis=1).astype(BF16)
    dils = [dil for _, dil in A_PAIRS[1:]]
    w_dil = [jnp.concatenate(a_cols(g), axis=1).astype(BF16) for g in range(1, A_GROUPS)]
    row_w = w_nat.shape[1]
    col_aq = N_GATES * d
    col_bq = col_aq + 3 * A_GROUP_WIDTH
    col_bk = col_bq + B_Q_WIDTH
    col_bv = col_bk + B_KV_WIDTH
    col_cq = col_bv + B_KV_WIDTH

    x2 = x.reshape(t, d)
    h2, *h_dil = _in_proj(x2, bsz, w_nat, w_dil, dils, 256)
    h3 = h2.reshape(bsz, s, row_w)

    bias = _bias_tiles(rel_bias)
    groups = []
    for g in range(A_GROUPS):
        if g == 0:
            hr, c0 = h3.reshape(bsz, 1, s, row_w), col_aq // A_GROUP_WIDTH
        else:
            hr, c0 = h_dil[g - 1], 0
        groups += _attn_a_group(hr, bias, g, c0, c0 + 1, c0 + 2)

    yb = _attn_b(h3, bias, sinks, col_bq, col_bk, col_bv).reshape(t, B_Q_WIDTH)

    m = mem.shape[1]
    kvm = _project(mem.reshape(bsz * m, d), w_mem_kv.astype(BF16), 256, 2 * C_WIDTH, "mem_kv")
    yc = _attn_c(h3, kvm.reshape(bsz, m, 2 * C_WIDTH), col_cq, 512).reshape(t, C_WIDTH)

    x1, x1b, q = _merge(alpha, groups, yb, yc, h2, x2, b_gate.reshape(1, -1).astype(F32),
                        w_a.astype(BF16), w_b.astype(BF16), w_c.astype(BF16), w_out.astype(BF16),
                        g1.reshape(1, d).astype(F32), b1.reshape(1, d).astype(F32),
                        w_query.astype(BF16), 512)

    ei, ej, gate = _retrieve(q, sub_keys.astype(BF16))
    gm = _gate_matrix(ei, ej, gate, 256)
    ec = 1024
    out = _experts(alpha, x1b, _chunk_order(u_tab, ec).astype(BF16).T,
                   _chunk_order(v_tab, ec).astype(BF16), gm, x1,
                   g2.reshape(1, d).astype(F32), b2.reshape(1, d).astype(F32), 1024, ec, 256)
    return out.reshape(bsz, s, d)


def kernel(x, mem, rel_bias, w_in, b_gate, w_mem_kv, sinks, w_branch_a, w_branch_b, w_branch_c,
           w_out, ln1_g, ln1_b, peer_w_query, peer_sub_keys, peer_u, peer_v, ln2_g, ln2_b):
    depth = w_in.shape[0]
    alpha = (2.0 * depth) ** 0.25
    for l in range(depth):
        x = _layer(x, mem, rel_bias, w_in[l], b_gate[l], w_mem_kv[l], sinks[l], w_branch_a[l],
                   w_branch_b[l], w_branch_c[l], w_out[l], ln1_g[l], ln1_b[l], peer_w_query[l],
                   peer_sub_keys[l], peer_u[l], peer_v[l], ln2_g[l], ln2_b[l], alpha)
    return x
```

```python
import functools
import math

import numpy as np
import jax
import jax.numpy as jnp
from jax import lax
from jax.experimental import pallas as pl
from jax.experimental.pallas import tpu as pltpu

F32 = jnp.float32
BF16 = jnp.bfloat16

HEAD_DIM = 64
BLOCK = 128
A_PAIRS = ((128, 1), (512, 4), (2048, 16))
A_GROUPS = 3
A_HEADS = 4
A_GROUP_WIDTH = A_HEADS * HEAD_DIM
A_WIDTH = A_GROUPS * A_GROUP_WIDTH
B_Q_HEADS = 8
B_KV_HEADS = 2
B_WINDOW = 128
B_Q_WIDTH = B_Q_HEADS * HEAD_DIM
B_KV_WIDTH = B_KV_HEADS * HEAD_DIM
C_HEADS = 4
C_HEAD_DIM = 128
C_WIDTH = C_HEADS * C_HEAD_DIM
N_GATES = 3
N_BUCKETS = 32
MAX_DISTANCE = 2048
PEER_HEADS = 8
N_KEYS = 128
PEER_TOPK = 16
PEER_HALF = 64
LN_EPS = 1e-5
NEG = -1e30

V7X_VMEM_BYTES = 64 * 1024 * 1024
SUBLANES = 8
LANES = 128


def _cparams(n_grid, vmem_mb):
    return pltpu.CompilerParams(
        dimension_semantics=("arbitrary",) * n_grid,
        vmem_limit_bytes=vmem_mb * 1024 * 1024,
    )


def _dot(a, b):
    return jnp.dot(a, b, preferred_element_type=F32)


def _dot_nt(a, b):
    return lax.dot_general(a, b, (((1,), (1,)), ((), ())), preferred_element_type=F32)


def _layer_norm(y, g, b):
    mu = jnp.mean(y, axis=-1, keepdims=True)
    yc = y - mu
    var = jnp.mean(yc * yc, axis=-1, keepdims=True)
    return yc * lax.rsqrt(var + LN_EPS) * g + b


def _proj_kernel(x_ref, w_ref, o_ref):
    o_ref[...] = _dot(x_ref[...].astype(BF16), w_ref[...]).astype(o_ref.dtype)


def _project(x, w, tm, tn, name):
    m, k = x.shape
    n = w.shape[1]
    return pl.pallas_call(
        _proj_kernel,
        grid=(n // tn, m // tm),
        in_specs=[pl.BlockSpec((tm, k), lambda j, i: (i, 0)),
                  pl.BlockSpec((k, tn), lambda j, i: (0, j))],
        out_specs=pl.BlockSpec((tm, tn), lambda j, i: (i, j)),
        out_shape=jax.ShapeDtypeStruct((m, n), BF16),
        compiler_params=_cparams(2, 48),
        name=name,
    )(x, w)


def _in_proj_kernel(x_ref, wn_ref, w1_ref, w2_ref, p1_ref, p2_ref, hn_ref, h1_ref, h2_ref):
    tm = x_ref.shape[0]
    xb = x_ref[...].astype(BF16)
    hn_ref[...] = _dot(xb, wn_ref[...]).astype(hn_ref.dtype)
    for w_ref, p_ref, h_ref in ((w1_ref, p1_ref, h1_ref), (w2_ref, p2_ref, h2_ref)):
        r = h_ref.shape[1]
        h = _dot(xb, w_ref[...]).astype(BF16)
        hp = _dot(p_ref[...], h).astype(h_ref.dtype)
        h_ref[0] = hp.reshape(r, tm // r, hp.shape[-1])


def _class_major_permutation(tm, r):
    dst = np.arange(tm)
    src = (dst % (tm // r)) * r + dst // (tm // r)
    p = np.zeros((tm, tm), np.float32)
    p[dst, src] = 1.0
    return jnp.asarray(p, BF16)


def _in_proj(x2, bsz, w_nat, w_dil, dils, tm):
    t, d = x2.shape
    s = t // bsz
    nt = s // tm
    full = lambda a: pl.BlockSpec(a.shape, lambda b, i: (0, 0))
    dil_spec = lambda r, w: pl.BlockSpec((1, r, tm // r, w), lambda b, i: (b, 0, i, 0))
    perms = [_class_major_permutation(tm, r) for r in dils]
    return pl.pallas_call(
        _in_proj_kernel,
        grid=(bsz, nt),
        in_specs=[pl.BlockSpec((tm, d), lambda b, i: (b * nt + i, 0)),
                  full(w_nat), full(w_dil[0]), full(w_dil[1]), full(perms[0]), full(perms[1])],
        out_specs=[pl.BlockSpec((tm, w_nat.shape[1]), lambda b, i: (b * nt + i, 0)),
                   dil_spec(dils[0], w_dil[0].shape[1]), dil_spec(dils[1], w_dil[1].shape[1])],
        out_shape=[jax.ShapeDtypeStruct((t, w_nat.shape[1]), BF16),
                   jax.ShapeDtypeStruct((bsz, dils[0], s // dils[0], w_dil[0].shape[1]), BF16),
                   jax.ShapeDtypeStruct((bsz, dils[1], s // dils[1], w_dil[1].shape[1]), BF16)],
        compiler_params=_cparams(2, 56),
        name="in_proj",
    )(x2, w_nat, *w_dil, *perms)


def _t5_bucket(dist):
    n = np.asarray(dist, dtype=np.int32)
    max_exact = N_BUCKETS // 2
    nf = np.maximum(n, 1).astype(np.float32)
    scale = np.float32(math.log(MAX_DISTANCE / max_exact))
    large = max_exact + (np.log(nf / np.float32(max_exact)) / scale
                         * np.float32(N_BUCKETS - max_exact)).astype(np.int32)
    large = np.minimum(large, N_BUCKETS - 1)
    return np.where(n < max_exact, n, large).astype(np.int32)


def _bucket_tiles(dist_scale, max_off):
    i = np.arange(BLOCK)[:, None]
    j = np.arange(2 * BLOCK)[None, :]
    off = BLOCK + i - j
    bucket = _t5_bucket(np.clip(off, 0, max_off) * dist_scale)
    valid = (off >= 0) & (off <= max_off)
    general = np.where(valid, bucket, -1)
    first = np.where(valid & (j >= BLOCK), bucket, -1)
    return np.stack([first, general], axis=0).astype(np.int32)


def _bias_kernel(table_ref, bucket_ref, o_ref):
    h = pl.program_id(0)
    col = jnp.where(h < B_Q_HEADS, h + A_GROUPS * A_HEADS, h - B_Q_HEADS)
    for var in range(2):
        bk = bucket_ref[0, var]
        acc = jnp.full(bk.shape, NEG, F32)
        for b in range(N_BUCKETS):
            acc = jnp.where(bk == b, table_ref[b, col], acc)
        o_ref[var, 0] = acc


def _bias_tiles(rel_bias):
    kinds = [_bucket_tiles(1, B_WINDOW - 1)] + [_bucket_tiles(d, w // d) for w, d in A_PAIRS]
    buckets = jnp.asarray(np.stack(kinds, axis=0))
    n_heads = B_Q_HEADS + A_GROUPS * A_HEADS

    def kind(h):
        return jnp.where(h < B_Q_HEADS, 0, 1 + (h - B_Q_HEADS) // A_HEADS)

    return pl.pallas_call(
        _bias_kernel,
        grid=(n_heads,),
        in_specs=[pl.BlockSpec(memory_space=pltpu.SMEM),
                  pl.BlockSpec((1, 2, BLOCK, 2 * BLOCK), lambda h: (kind(h), 0, 0, 0))],
        out_specs=pl.BlockSpec((2, 1, BLOCK, 2 * BLOCK), lambda h: (0, h, 0, 0)),
        out_shape=jax.ShapeDtypeStruct((2, n_heads, BLOCK, 2 * BLOCK), F32),
        compiler_params=_cparams(1, 32),
        name="bias_tiles",
    )(rel_bias.astype(F32), buckets)


ATTN_BLOCKS_PER_STEP = 2


def _attn_a_kernel(r, q_ref, kp_ref, kc_ref, vp_ref, vc_ref, bias0_ref, bias_ref,
                   o0_ref, o1_ref, l0_ref, l1_ref):
    c = pl.program_id(2)
    q = q_ref[0, 0]
    kall = jnp.concatenate([kp_ref[0, 0], kc_ref[0, 0]], axis=0)
    vall = jnp.concatenate([vp_ref[0, 0], vc_ref[0, 0]], axis=0)
    scale = HEAD_DIM ** -0.5
    low = lax.broadcasted_iota(jnp.int32, (1, LANES), 1) < HEAD_DIM
    zero = jnp.zeros((), BF16)
    masks = (low, jnp.logical_not(low))
    pairs = [slice(LANES * pr, LANES * (pr + 1)) for pr in range(A_HEADS // 2)]
    for j in range(q.shape[0] // BLOCK):
        qj = q[BLOCK * j:BLOCK * (j + 1)]
        kcat = kall[BLOCK * j:BLOCK * (j + 2)]
        vcat = vall[BLOCK * j:BLOCK * (j + 2)]
        bias = (bias0_ref if j == 0 else bias_ref)[0]
        rows = pl.ds(BLOCK * r * j + c, BLOCK, stride=r) if r > 1 else slice(BLOCK * j, BLOCK * (j + 1))
        logits = jnp.concatenate(
            [_dot_nt(qj[:, sl], jnp.where(mk, kcat[:, sl], zero)) for sl in pairs for mk in masks],
            axis=0)
        logits = logits * scale + bias.reshape(A_HEADS * BLOCK, 2 * BLOCK)
        m = jnp.max(logits, axis=-1, keepdims=True)
        e = jnp.exp(logits - m)
        s = jnp.sum(e, axis=-1, keepdims=True)
        p = (e * (1.0 / s)).astype(BF16)
        lse = m + jnp.log(s)
        for pr, (sl, o_ref, l_ref) in enumerate(zip(pairs, (o0_ref, o1_ref), (l0_ref, l1_ref))):
            out = None
            for hf, mk in enumerate(masks):
                h = 2 * pr + hf
                part = _dot(p[BLOCK * h:BLOCK * (h + 1)], jnp.where(mk, vcat[:, sl], zero))
                out = part if out is None else out + part
            o_ref[rows, :] = out
            l_ref[rows, :] = jnp.where(low, lse[BLOCK * 2 * pr:BLOCK * (2 * pr + 1)],
                                       lse[BLOCK * (2 * pr + 1):BLOCK * (2 * pr + 2)])


def _attn_a_group(hr, bias, g, col_q, col_k, col_v):
    bsz, r, ln, _ = hr.shape
    nb = min(2 * ATTN_BLOCKS_PER_STEP, ln // BLOCK)
    span = nb * BLOCK
    nspan = ln // span
    half = A_GROUP_WIDTH // 2

    def spec(col, prev):
        if prev:
            return pl.BlockSpec((1, 1, BLOCK, A_GROUP_WIDTH),
                                lambda b, n, c: (b, c, jnp.maximum(n * nb - 1, 0), col))
        return pl.BlockSpec((1, 1, span, A_GROUP_WIDTH), lambda b, n, c: (b, c, n, col))

    def bias_spec(first_aware):
        return pl.BlockSpec(
            (1, A_HEADS, BLOCK, 2 * BLOCK),
            lambda b, n, c: (jnp.minimum(n, 1) if first_aware else 1, B_Q_HEADS // A_HEADS + g, 0, 0))

    out_spec = pl.BlockSpec((span * r, half), lambda b, n, c: (b * nspan + n, 0))
    return pl.pallas_call(
        functools.partial(_attn_a_kernel, r),
        grid=(bsz, nspan, r),
        in_specs=[spec(col_q, False), spec(col_k, True), spec(col_k, False),
                  spec(col_v, True), spec(col_v, False), bias_spec(True), bias_spec(False)],
        out_specs=[out_spec] * 4,
        out_shape=[jax.ShapeDtypeStruct((bsz * ln * r, half), F32)] * 4,
        compiler_params=_cparams(3, 32),
        name=f"attn_a{g}",
    )(hr, hr, hr, hr, hr, bias, bias)


def _attn_b_kernel(sinks_ref, q0_ref, q1_ref, kp_ref, kc_ref, vp_ref, vc_ref, bias0_ref, bias_ref,
                   o_ref):
    kall = jnp.concatenate([kp_ref[0], kc_ref[0]], axis=0)
    vall = jnp.concatenate([vp_ref[0], vc_ref[0]], axis=0)
    scale = HEAD_DIM ** -0.5
    per_kv = B_Q_HEADS // B_KV_HEADS
    for j in range(o_ref.shape[1] // BLOCK):
        rows = slice(BLOCK * j, BLOCK * (j + 1))
        kcat = kall[BLOCK * j:BLOCK * (j + 2)]
        vcat = vall[BLOCK * j:BLOCK * (j + 2)]
        bias = bias0_ref if j == 0 else bias_ref
        outs = []
        for kv, q_ref in enumerate((q0_ref, q1_ref)):
            q = q_ref[0, rows, :]
            ks = kcat[:, HEAD_DIM * kv:HEAD_DIM * (kv + 1)]
            vs = vcat[:, HEAD_DIM * kv:HEAD_DIM * (kv + 1)]
            heads = range(per_kv * kv, per_kv * (kv + 1))
            logits = jnp.concatenate(
                [_dot_nt(q[:, HEAD_DIM * n:HEAD_DIM * (n + 1)], ks) for n in range(per_kv)], axis=0)
            logits = logits * scale + bias[0, per_kv * kv:per_kv * (kv + 1)].reshape(
                per_kv * BLOCK, 2 * BLOCK)
            sink = jnp.concatenate([jnp.full((BLOCK, 1), sinks_ref[h], F32) for h in heads], axis=0)
            m = jnp.maximum(jnp.max(logits, axis=-1, keepdims=True), sink)
            e = jnp.exp(logits - m)
            denom = jnp.sum(e, axis=-1, keepdims=True) + jnp.exp(sink - m)
            p = (e * (1.0 / denom)).astype(BF16)
            outs += [_dot(p[BLOCK * n:BLOCK * (n + 1)], vs) for n in range(per_kv)]
        o_ref[0, rows, :] = jnp.concatenate(outs, axis=1).astype(o_ref.dtype)


def _attn_b(h3, bias, sinks, col_q, col_k, col_v):
    bsz, s, _ = h3.shape
    nb = ATTN_BLOCKS_PER_STEP
    span = nb * BLOCK
    qw = B_Q_WIDTH // B_KV_HEADS
    cq = col_q // qw
    ck = col_k // B_KV_WIDTH
    cv = col_v // B_KV_WIDTH

    def kv_spec(col, prev):
        if prev:
            return pl.BlockSpec((1, BLOCK, B_KV_WIDTH),
                                lambda b, n: (b, jnp.maximum(n * nb - 1, 0), col))
        return pl.BlockSpec((1, span, B_KV_WIDTH), lambda b, n: (b, n, col))

    def bias_spec(first_aware):
        return pl.BlockSpec((1, B_Q_HEADS, BLOCK, 2 * BLOCK),
                            lambda b, n: (jnp.minimum(n, 1) if first_aware else 1, 0, 0, 0))

    return pl.pallas_call(
        _attn_b_kernel,
        grid=(bsz, s // span),
        in_specs=[pl.BlockSpec(memory_space=pltpu.SMEM),
                  pl.BlockSpec((1, span, qw), lambda b, n: (b, n, cq)),
                  pl.BlockSpec((1, span, qw), lambda b, n: (b, n, cq + 1)),
                  kv_spec(ck, True), kv_spec(ck, False), kv_spec(cv, True), kv_spec(cv, False),
                  bias_spec(True), bias_spec(False)],
        out_specs=pl.BlockSpec((1, span, B_Q_WIDTH), lambda b, n: (b, n, 0)),
        out_shape=jax.ShapeDtypeStruct((bsz, s, B_Q_WIDTH), BF16),
        compiler_params=_cparams(2, 32),
        name="attn_b",
    )(sinks.astype(F32), h3, h3, h3, h3, h3, h3, bias, bias)


def _attn_c_kernel(q_ref, k_ref, v_ref, o_ref):
    q = q_ref[0]
    k = k_ref[0]
    v = v_ref[0]
    scale = C_HEAD_DIM ** -0.5
    for h in range(C_HEADS):
        sl = slice(C_HEAD_DIM * h, C_HEAD_DIM * (h + 1))
        logits = _dot_nt(q[:, sl], k[:, sl]) * scale
        m = jnp.max(logits, axis=-1, keepdims=True)
        e = jnp.exp(logits - m)
        inv = 1.0 / jnp.sum(e, axis=-1, keepdims=True)
        o_ref[0, :, sl] = _dot((e * inv).astype(BF16), v[:, sl]).astype(o_ref.dtype)


def _attn_c(h3, kvm, col_q, tq):
    bsz, s, _ = h3.shape
    m = kvm.shape[1]
    cq = col_q // C_WIDTH
    return pl.pallas_call(
        _attn_c_kernel,
        grid=(bsz, s // tq),
        in_specs=[pl.BlockSpec((1, tq, C_WIDTH), lambda b, n: (b, n, cq)),
                  pl.BlockSpec((1, m, C_WIDTH), lambda b, n: (b, 0, 0)),
                  pl.BlockSpec((1, m, C_WIDTH), lambda b, n: (b, 0, 1))],
        out_specs=pl.BlockSpec((1, tq, C_WIDTH), lambda b, n: (b, n, 0)),
        out_shape=jax.ShapeDtypeStruct((bsz, s, C_WIDTH), BF16),
        compiler_params=_cparams(2, 32),
        name="attn_c",
    )(h3, kvm, kvm)


def _merge_kernel(alpha, *refs):
    group_refs, refs = refs[:4 * A_GROUPS], refs[4 * A_GROUPS:]
    (yb_ref, yc_ref, gate_ref, x_ref, bg_ref, wa_ref, wb_ref, wc_ref, wo_ref, g1_ref, b1_ref,
     wq_ref, x1_ref, x1b_ref, q_ref) = refs
    d = x_ref.shape[-1]
    halves = []
    for hf in range(2):
        o0, o1, o2 = (group_refs[4 * g + hf][...] for g in range(A_GROUPS))
        la, lb, lc = (group_refs[4 * g + 2 + hf][...] for g in range(A_GROUPS))
        m = jnp.maximum(jnp.maximum(la, lb), lc)
        wa, wb, wc = jnp.exp(la - m), jnp.exp(lb - m), jnp.exp(lc - m)
        inv = 1.0 / (wa + wb + wc)
        halves.append((wa * inv) * o0 + (wb * inv) * o1 + (wc * inv) * o2)
    ya = jnp.concatenate(halves, axis=1)

    def gate(n):
        pre = gate_ref[:, n * d:(n + 1) * d].astype(F32) + bg_ref[:, n * d:(n + 1) * d]
        return jax.nn.sigmoid(pre)

    merged = (gate(0) * _dot(ya.astype(BF16), wa_ref[...])
              + gate(1) * _dot(yb_ref[...], wb_ref[...])
              + gate(2) * _dot(yc_ref[...], wc_ref[...]))
    y = alpha * x_ref[...] + _dot(merged.astype(BF16), wo_ref[...])
    x1 = _layer_norm(y, g1_ref[...], b1_ref[...])
    x1_ref[...] = x1
    x1b = x1.astype(BF16)
    x1b_ref[...] = x1b
    q_ref[...] = _dot(x1b, wq_ref[...]).astype(q_ref.dtype)


def _merge(alpha, groups, yb, yc, h2, x2, b_gate, w_a, w_b, w_c, w_o, g1, b1, w_q, tm):
    t, d = x2.shape
    row = lambda w: pl.BlockSpec((tm, w), lambda i: (i, 0))
    full = lambda a: pl.BlockSpec(a.shape, lambda i: (0,) * a.ndim)
    weights = [b_gate, w_a, w_b, w_c, w_o, g1, b1, w_q]
    ins = [*groups, yb, yc, h2, x2, *weights]
    in_specs = ([row(A_GROUP_WIDTH // 2)] * len(groups)
                + [row(B_Q_WIDTH), row(C_WIDTH), row(N_GATES * d), row(d)]
                + [full(a) for a in weights])
    return pl.pallas_call(
        functools.partial(_merge_kernel, alpha),
        grid=(t // tm,),
        in_specs=in_specs,
        out_specs=[row(d), row(d), row(w_q.shape[1])],
        out_shape=[jax.ShapeDtypeStruct((t, d), F32), jax.ShapeDtypeStruct((t, d), BF16),
                   jax.ShapeDtypeStruct((t, w_q.shape[1]), BF16)],
        compiler_params=_cparams(1, 48),
        name="merge_ln1",
    )(*ins)


def _cmpx(a, b, need_lo=True):
    if b is None:
        return a, None
    if a is None:
        return b, None
    (va, ta), (vb, tb) = a, b
    a_first = (va > vb) | ((va == vb) & (ta < tb))
    hi = (jnp.maximum(va, vb), jnp.where(a_first, ta, tb))
    lo = (jnp.minimum(va, vb), jnp.where(a_first, tb, ta)) if need_lo else None
    return hi, lo


def _oddeven_sort_pairs(n):
    pairs = []

    def merge(lo, m, r):
        step = 2 * r
        if step < m:
            merge(lo, m, step)
            merge(lo + r, m, step)
            pairs.extend((i, i + r) for i in range(lo + r, lo + m - r, step))
        else:
            pairs.append((lo, lo + r))

    def sort(lo, m):
        if m > 1:
            sort(lo, m // 2)
            sort(lo + m // 2, m // 2)
            merge(lo, m, 1)

    sort(0, n)
    return pairs


def _sort_items(items):
    items = list(items)
    for i, j in _oddeven_sort_pairs(len(items)):
        items[i], items[j] = _cmpx(items[i], items[j])
    return items


def _bitonic_merge(seq):
    n = len(seq)
    if n == 1:
        return seq
    firsts, seconds = zip(*(_cmpx(seq[i], seq[i + n // 2]) for i in range(n // 2)))
    return _bitonic_merge(list(firsts)) + _bitonic_merge(list(seconds))


def _merge_top(a, b, k):
    p = 1
    while p < len(a) + len(b):
        p *= 2
    seq = list(a) + [None] * (p - len(a) - len(b)) + list(b)[::-1]
    while len(seq) > k:
        half = len(seq) // 2
        seq = [_cmpx(seq[i], seq[i + half], need_lo=False)[0] for i in range(half)]
    return [x for x in _bitonic_merge(seq) if x is not None][:k]


def _top_of_keys(items, k):
    runs = [_sort_items(items[i:i + k]) for i in range(0, len(items), k)]
    while len(runs) > 1:
        runs = [_merge_top(runs[i], runs[i + 1], k) for i in range(0, len(runs), 2)]
    return runs[0]


RETRIEVE_TOKENS = SUBLANES * LANES
SCORE_PITCH = N_KEYS + SUBLANES


def _tile_const(x):
    return jnp.full((SUBLANES, LANES), float(x), F32)


def _select_pairs(scores1, scores2):
    halves = [_top_of_keys([(s, _tile_const(k)) for k, s in enumerate(scores)], PEER_TOPK)
              for scores in (scores1, scores2)]
    (v1, i1), (v2, i2) = (tuple(zip(*half)) for half in halves)
    pair = lambda k1, k2: (v1[k1] + v2[k2], _tile_const(k1 * PEER_TOPK + k2))
    rows = [[pair(k1, k2) for k2 in range(PEER_TOPK // (k1 + 1))] for k1 in range(8)]
    column = [pair(k1, 0) for k1 in range(8, PEER_TOPK)]
    small = _merge_top(_merge_top(rows[5], rows[6], PEER_TOPK),
                       _merge_top(rows[7], rows[4], PEER_TOPK), PEER_TOPK)
    mid = _merge_top(_merge_top(rows[3], rows[2], PEER_TOPK), small, PEER_TOPK)
    best = _merge_top(rows[0], _merge_top(mid, _merge_top(rows[1], column, PEER_TOPK),
                                          PEER_TOPK), PEER_TOPK)
    top, pos = zip(*best)
    e = [jnp.exp(t - top[0]) for t in top]
    inv = 1.0 / functools.reduce(lambda a, b: a + b, e)
    out_i, out_j, out_g = [], [], []
    for k in range(PEER_TOPK):
        k1 = jnp.floor(pos[k] * (1.0 / PEER_TOPK))
        k2 = pos[k] - k1 * PEER_TOPK
        ei, ej = i1[0], i2[0]
        for r in range(1, PEER_TOPK):
            ei = jnp.where(k1 == r, i1[r], ei)
            ej = jnp.where(k2 == r, i2[r], ej)
        out_i.append(ei)
        out_j.append(ej)
        out_g.append(e[k] * inv)
    return out_i, out_j, out_g


def _retrieve_kernel(q_ref, keys_ref, i_ref, j_ref, g_ref, s_ref, ri_ref, rj_ref, rg_ref):
    def per_head(h, carry):
        col = pl.multiple_of(h * (2 * PEER_HALF), LANES)
        scores = []
        for c in range(2):
            keys = keys_ref[h, c]
            for ch in range(SUBLANES):
                qc = q_ref[ch * LANES:(ch + 1) * LANES, pl.ds(col, LANES)]
                s_ref[c, ch * SCORE_PITCH:ch * SCORE_PITCH + N_KEYS, :] = _dot_nt(
                    keys, qc[:, c * PEER_HALF:(c + 1) * PEER_HALF])
            scores.append([s_ref.at[c][pl.ds(k, SUBLANES, stride=SCORE_PITCH), :]
                           for k in range(N_KEYS)])
        out_i, out_j, out_g = _select_pairs(*scores)
        for k in range(PEER_TOPK):
            row0 = pl.multiple_of((h * PEER_TOPK + k) * SUBLANES, SUBLANES)
            ri_ref[pl.ds(row0, SUBLANES), :] = out_i[k]
            rj_ref[pl.ds(row0, SUBLANES), :] = out_j[k]
            rg_ref[pl.ds(row0, SUBLANES), :] = out_g[k]
        return carry

    lax.fori_loop(0, PEER_HEADS, per_head, 0)
    for ch in range(SUBLANES):
        rows = slice(ch * LANES, (ch + 1) * LANES)
        for r_ref, o_ref in ((ri_ref, i_ref), (rj_ref, j_ref), (rg_ref, g_ref)):
            o_ref[rows, :] = r_ref[pl.ds(ch, PEER_HEADS * PEER_TOPK, stride=SUBLANES), :].T


def _retrieve(q, keys):
    t, w = q.shape
    tt = RETRIEVE_TOKENS
    npair = PEER_HEADS * PEER_TOPK
    out = pl.BlockSpec((tt, npair), lambda i: (i, 0))
    slots = pltpu.VMEM((npair * SUBLANES, LANES), F32)
    return pl.pallas_call(
        _retrieve_kernel,
        grid=(t // tt,),
        in_specs=[pl.BlockSpec((tt, w), lambda i: (i, 0)),
                  pl.BlockSpec(keys.shape, lambda i: (0, 0, 0, 0))],
        out_specs=[out, out, out],
        out_shape=[jax.ShapeDtypeStruct((t, npair), F32)] * 3,
        scratch_shapes=[pltpu.VMEM((2, SUBLANES * SCORE_PITCH, LANES), F32), slots, slots, slots],
        compiler_params=_cparams(1, 32),
        name="peer_retrieve",
    )(q, keys)


GATE_TOKENS_PER_TRIP = 64


def _gate_matrix_kernel(i_ref, j_ref, g_ref, o_ref):
    tt, npair = i_ref.shape
    key = lax.broadcasted_iota(jnp.int32, (N_KEYS, npair), 0).astype(F32).astype(BF16)
    one = jnp.ones((N_KEYS, npair), BF16)
    zero = jnp.zeros((N_KEYS, npair), BF16)

    def per_trip(trip, carry):
        for u in range(GATE_TOKENS_PER_TRIP):
            t = trip * GATE_TOKENS_PER_TRIP + u
            irow = jnp.broadcast_to(i_ref[pl.ds(t, 1), :], (N_KEYS, npair)).astype(BF16)
            jrow = jnp.broadcast_to(j_ref[pl.ds(t, 1), :], (N_KEYS, npair)).astype(BF16)
            grow = jnp.broadcast_to(g_ref[pl.ds(t, 1), :], (N_KEYS, npair)).astype(BF16)
            a = jnp.where(key == irow, grow, zero)
            b = jnp.where(key == jrow, one, zero)
            grp, s = divmod(u, SUBLANES)
            start = (trip * (GATE_TOKENS_PER_TRIP // SUBLANES) + grp) * (SUBLANES * N_KEYS) + s
            o_ref[pl.ds(start, N_KEYS, stride=SUBLANES), :] = _dot_nt(a, b)
        return carry

    lax.fori_loop(0, tt // GATE_TOKENS_PER_TRIP, per_trip, 0)


def _gate_matrix(ei, ej, gate, tt):
    t, npair = ei.shape
    spec = pl.BlockSpec((tt, npair), lambda i: (i, 0))
    gm = pl.pallas_call(
        _gate_matrix_kernel,
        grid=(t // tt,),
        in_specs=[spec, spec, spec],
        out_specs=pl.BlockSpec((tt * N_KEYS, N_KEYS), lambda i: (i, 0)),
        out_shape=jax.ShapeDtypeStruct((t * N_KEYS, N_KEYS), F32),
        compiler_params=_cparams(1, 48),
        name="peer_gate_matrix",
    )(ei, ej, gate)
    return gm.reshape(t // SUBLANES, N_KEYS, SUBLANES, N_KEYS)


def _experts_kernel(alpha, sub, xb_ref, ut_ref, v_ref, gm_ref, x1_ref, g2_ref, b2_ref, o_ref, acc_ref):
    j = pl.program_id(1)

    @pl.when(j == 0)
    def _():
        acc_ref[...] = jnp.zeros_like(acc_ref)

    tt = xb_ref.shape[0]
    n_i = gm_ref.shape[1]
    for r0 in range(0, tt, sub):
        hid = _dot(xb_ref[r0:r0 + sub, :], ut_ref[...])
        act = 0.5 * hid * (1.0 + lax.erf(hid * (2.0 ** -0.5)))
        grp = slice(r0 // SUBLANES, (r0 + sub) // SUBLANES)
        gate = jnp.concatenate([gm_ref[grp, i].reshape(sub, N_KEYS) for i in range(n_i)], axis=1)
        acc_ref[r0:r0 + sub, :] += _dot((gate * act).astype(BF16), v_ref[...])

    @pl.when(j == pl.num_programs(1) - 1)
    def _():
        y = alpha * x1_ref[...] + acc_ref[...]
        o_ref[...] = _layer_norm(y, g2_ref[...], b2_ref[...])


def _experts(alpha, x1b, ut, v, gm, x1, g2, b2, tt, ec, sub):
    t, d = x1.shape
    ne = ut.shape[1]
    return pl.pallas_call(
        functools.partial(_experts_kernel, alpha, sub),
        grid=(t // tt, ne // ec),
        in_specs=[pl.BlockSpec((tt, d), lambda i, j: (i, 0)),
                  pl.BlockSpec((d, ec), lambda i, j: (0, j)),
                  pl.BlockSpec((ec, d), lambda i, j: (j, 0)),
                  pl.BlockSpec((tt // SUBLANES, ec // N_KEYS, SUBLANES, N_KEYS),
                               lambda i, j: (i, j, 0, 0)),
                  pl.BlockSpec((tt, d), lambda i, j: (i, 0)),
                  pl.BlockSpec((1, d), lambda i, j: (0, 0)),
                  pl.BlockSpec((1, d), lambda i, j: (0, 0))],
        out_specs=pl.BlockSpec((tt, d), lambda i, j: (i, 0)),
        out_shape=jax.ShapeDtypeStruct((t, d), F32),
        scratch_shapes=[pltpu.VMEM((tt, d), F32)],
        compiler_params=_cparams(2, 56),
        name="peer_experts",
    )(x1b, ut, v, gm, x1, g2, b2)


def _layer(x, mem, rel_bias, w_in, b_gate, w_mem_kv, sinks, w_a, w_b, w_c, w_out, g1, b1,
           w_query, sub_keys, u_tab, v_tab, g2, b2, alpha):
    bsz, s, d = x.shape
    t = bsz * s

    def a_cols(g):
        return [w_in[:, part * A_WIDTH + g * A_GROUP_WIDTH:part * A_WIDTH + (g + 1) * A_GROUP_WIDTH]
                for part in range(3)]

    rest_at = 3 * A_WIDTH
    gates_at = rest_at + B_Q_WIDTH + 2 * B_KV_WIDTH + C_WIDTH
    w_nat = jnp.concatenate([w_in[:, gates_at:], *a_cols(0), w_in[:, rest_at:gates_at]],
                            axis=1).astype(BF16)
    dils = [dil for _, dil in A_PAIRS[1:]]
    w_dil = [jnp.concatenate(a_cols(g), axis=1).astype(BF16) for g in range(1, A_GROUPS)]
    row_w = w_nat.shape[1]
    col_aq = N_GATES * d
    col_bq = col_aq + 3 * A_GROUP_WIDTH
    col_bk = col_bq + B_Q_WIDTH
    col_bv = col_bk + B_KV_WIDTH
    col_cq = col_bv + B_KV_WIDTH

    x2 = x.reshape(t, d)
    h2, *h_dil = _in_proj(x2, bsz, w_nat, w_dil, dils, 256)
    h3 = h2.reshape(bsz, s, row_w)

    bias = _bias_tiles(rel_bias)
    groups = []
    for g in range(A_GROUPS):
        if g == 0:
            hr, c0 = h3.reshape(bsz, 1, s, row_w), col_aq // A_GROUP_WIDTH
        else:
            hr, c0 = h_dil[g - 1], 0
        groups += _attn_a_group(hr, bias, g, c0, c0 + 1, c0 + 2)

    yb = _attn_b(h3, bias, sinks, col_bq, col_bk, col_bv).reshape(t, B_Q_WIDTH)

    m = mem.shape[1]
    kvm = _project(mem.reshape(bsz * m, d), w_mem_kv.astype(BF16), 256, 2 * C_WIDTH, "mem_kv")
    yc = _attn_c(h3, kvm.reshape(bsz, m, 2 * C_WIDTH), col_cq, 512).reshape(t, C_WIDTH)

    x1, x1b, q = _merge(alpha, groups, yb, yc, h2, x2, b_gate.reshape(1, -1).astype(F32),
                        w_a.astype(BF16), w_b.astype(BF16), w_c.astype(BF16), w_out.astype(BF16),
                        g1.reshape(1, d).astype(F32), b1.reshape(1, d).astype(F32),
                        w_query.astype(BF16), 512)

    ei, ej, gate = _retrieve(q, sub_keys.astype(BF16))
    gm = _gate_matrix(ei, ej, gate, 256)
    out = _experts(alpha, x1b, u_tab.astype(BF16).T, v_tab.astype(BF16), gm, x1,
                   g2.reshape(1, d).astype(F32), b2.reshape(1, d).astype(F32), 1024, 1024, 256)
    return out.reshape(bsz, s, d)


def kernel(x, mem, rel_bias, w_in, b_gate, w_mem_kv, sinks, w_branch_a, w_branch_b, w_branch_c,
           w_out, ln1_g, ln1_b, peer_w_query, peer_sub_keys, peer_u, peer_v, ln2_g, ln2_b):
    depth = w_in.shape[0]
    alpha = (2.0 * depth) ** 0.25
    for l in range(depth):
        x = _layer(x, mem, rel_bias, w_in[l], b_gate[l], w_mem_kv[l], sinks[l], w_branch_a[l],
                   w_branch_b[l], w_branch_c[l], w_out[l], ln1_g[l], ln1_b[l], peer_w_query[l],
                   peer_sub_keys[l], peer_u[l], peer_v[l], ln2_g[l], ln2_b[l], alpha)
    return x
```

```python
import functools
import math

import numpy as np
import jax
import jax.numpy as jnp
from jax import lax
from jax.experimental import pallas as pl
from jax.experimental.pallas import tpu as pltpu

F32 = jnp.float32
BF16 = jnp.bfloat16

HEAD_DIM = 64
BLOCK = 128
A_PAIRS = ((128, 1), (512, 4), (2048, 16))
A_GROUPS = 3
A_HEADS = 4
A_GROUP_WIDTH = A_HEADS * HEAD_DIM
A_WIDTH = A_GROUPS * A_GROUP_WIDTH
B_Q_HEADS = 8
B_KV_HEADS = 2
B_WINDOW = 128
B_Q_WIDTH = B_Q_HEADS * HEAD_DIM
B_KV_WIDTH = B_KV_HEADS * HEAD_DIM
C_HEADS = 4
C_HEAD_DIM = 128
C_WIDTH = C_HEADS * C_HEAD_DIM
N_GATES = 3
N_BUCKETS = 32
MAX_DISTANCE = 2048
PEER_HEADS = 8
N_KEYS = 128
PEER_TOPK = 16
PEER_HALF = 64
LN_EPS = 1e-5
NEG = -1e30

V7X_VMEM_BYTES = 64 * 1024 * 1024
SUBLANES = 8
LANES = 128


def _cparams(n_grid, vmem_mb):
    return pltpu.CompilerParams(
        dimension_semantics=("arbitrary",) * n_grid,
        vmem_limit_bytes=vmem_mb * 1024 * 1024,
    )


def _dot(a, b):
    return jnp.dot(a, b, preferred_element_type=F32)


def _dot_nt(a, b):
    return lax.dot_general(a, b, (((1,), (1,)), ((), ())), preferred_element_type=F32)


def _layer_norm(y, g, b):
    mu = jnp.mean(y, axis=-1, keepdims=True)
    yc = y - mu
    var = jnp.mean(yc * yc, axis=-1, keepdims=True)
    return yc * lax.rsqrt(var + LN_EPS) * g + b


def _proj_kernel(x_ref, w_ref, o_ref):
    o_ref[...] = _dot(x_ref[...].astype(BF16), w_ref[...]).astype(o_ref.dtype)


def _project(x, w, tm, tn, name):
    m, k = x.shape
    n = w.shape[1]
    return pl.pallas_call(
        _proj_kernel,
        grid=(n // tn, m // tm),
        in_specs=[pl.BlockSpec((tm, k), lambda j, i: (i, 0)),
                  pl.BlockSpec((k, tn), lambda j, i: (0, j))],
        out_specs=pl.BlockSpec((tm, tn), lambda j, i: (i, j)),
        out_shape=jax.ShapeDtypeStruct((m, n), BF16),
        compiler_params=_cparams(2, 48),
        name=name,
    )(x, w)


def _in_proj_kernel(x_ref, wn_ref, w1_ref, w2_ref, p1_ref, p2_ref, hn_ref, h1_ref, h2_ref):
    tm = x_ref.shape[0]
    xb = x_ref[...].astype(BF16)
    hn_ref[...] = _dot(xb, wn_ref[...]).astype(hn_ref.dtype)
    for w_ref, p_ref, h_ref in ((w1_ref, p1_ref, h1_ref), (w2_ref, p2_ref, h2_ref)):
        r = h_ref.shape[1]
        h = _dot(xb, w_ref[...]).astype(BF16)
        hp = _dot(p_ref[...], h).astype(h_ref.dtype)
        h_ref[0] = hp.reshape(r, tm // r, hp.shape[-1])


def _class_major_permutation(tm, r):
    dst = np.arange(tm)
    src = (dst % (tm // r)) * r + dst // (tm // r)
    p = np.zeros((tm, tm), np.float32)
    p[dst, src] = 1.0
    return jnp.asarray(p, BF16)


def _in_proj(x2, bsz, w_nat, w_dil, dils, tm):
    t, d = x2.shape
    s = t // bsz
    nt = s // tm
    full = lambda a: pl.BlockSpec(a.shape, lambda b, i: (0, 0))
    dil_spec = lambda r, w: pl.BlockSpec((1, r, tm // r, w), lambda b, i: (b, 0, i, 0))
    perms = [_class_major_permutation(tm, r) for r in dils]
    return pl.pallas_call(
        _in_proj_kernel,
        grid=(bsz, nt),
        in_specs=[pl.BlockSpec((tm, d), lambda b, i: (b * nt + i, 0)),
                  full(w_nat), full(w_dil[0]), full(w_dil[1]), full(perms[0]), full(perms[1])],
        out_specs=[pl.BlockSpec((tm, w_nat.shape[1]), lambda b, i: (b * nt + i, 0)),
                   dil_spec(dils[0], w_dil[0].shape[1]), dil_spec(dils[1], w_dil[1].shape[1])],
        out_shape=[jax.ShapeDtypeStruct((t, w_nat.shape[1]), BF16),
                   jax.ShapeDtypeStruct((bsz, dils[0], s // dils[0], w_dil[0].shape[1]), BF16),
                   jax.ShapeDtypeStruct((bsz, dils[1], s // dils[1], w_dil[1].shape[1]), BF16)],
        compiler_params=_cparams(2, 56),
        name="in_proj",
    )(x2, w_nat, *w_dil, *perms)


def _t5_bucket(dist):
    n = np.asarray(dist, dtype=np.int32)
    max_exact = N_BUCKETS // 2
    nf = np.maximum(n, 1).astype(np.float32)
    scale = np.float32(math.log(MAX_DISTANCE / max_exact))
    large = max_exact + (np.log(nf / np.float32(max_exact)) / scale
                         * np.float32(N_BUCKETS - max_exact)).astype(np.int32)
    large = np.minimum(large, N_BUCKETS - 1)
    return np.where(n < max_exact, n, large).astype(np.int32)


def _bucket_tiles(dist_scale, max_off):
    i = np.arange(BLOCK)[:, None]
    j = np.arange(2 * BLOCK)[None, :]
    off = BLOCK + i - j
    bucket = _t5_bucket(np.clip(off, 0, max_off) * dist_scale)
    valid = (off >= 0) & (off <= max_off)
    general = np.where(valid, bucket, -1)
    first = np.where(valid & (j >= BLOCK), bucket, -1)
    return np.stack([first, general], axis=0).astype(np.int32)


def _bias_kernel(table_ref, bucket_ref, o_ref):
    h = pl.program_id(0)
    col = jnp.where(h < B_Q_HEADS, h + A_GROUPS * A_HEADS, h - B_Q_HEADS)
    for var in range(2):
        bk = bucket_ref[0, var]
        acc = jnp.full(bk.shape, NEG, F32)
        for b in range(N_BUCKETS):
            acc = jnp.where(bk == b, table_ref[b, col], acc)
        o_ref[var, 0] = acc


def _bias_tiles(rel_bias):
    kinds = [_bucket_tiles(1, B_WINDOW - 1)] + [_bucket_tiles(d, w // d) for w, d in A_PAIRS]
    buckets = jnp.asarray(np.stack(kinds, axis=0))
    n_heads = B_Q_HEADS + A_GROUPS * A_HEADS

    def kind(h):
        return jnp.where(h < B_Q_HEADS, 0, 1 + (h - B_Q_HEADS) // A_HEADS)

    return pl.pallas_call(
        _bias_kernel,
        grid=(n_heads,),
        in_specs=[pl.BlockSpec(memory_space=pltpu.SMEM),
                  pl.BlockSpec((1, 2, BLOCK, 2 * BLOCK), lambda h: (kind(h), 0, 0, 0))],
        out_specs=pl.BlockSpec((2, 1, BLOCK, 2 * BLOCK), lambda h: (0, h, 0, 0)),
        out_shape=jax.ShapeDtypeStruct((2, n_heads, BLOCK, 2 * BLOCK), F32),
        compiler_params=_cparams(1, 32),
        name="bias_tiles",
    )(rel_bias.astype(F32), buckets)


ATTN_BLOCKS_PER_STEP = 2


def _attn_a_kernel(r, q_ref, kp_ref, kc_ref, vp_ref, vc_ref, bias0_ref, bias_ref,
                   o0_ref, o1_ref, l0_ref, l1_ref):
    c = pl.program_id(2)
    q = q_ref[0, 0]
    kall = jnp.concatenate([kp_ref[0, 0], kc_ref[0, 0]], axis=0)
    vall = jnp.concatenate([vp_ref[0, 0], vc_ref[0, 0]], axis=0)
    scale = HEAD_DIM ** -0.5
    low = lax.broadcasted_iota(jnp.int32, (1, LANES), 1) < HEAD_DIM
    zero = jnp.zeros((), BF16)
    masks = (low, jnp.logical_not(low))
    pairs = [slice(LANES * pr, LANES * (pr + 1)) for pr in range(A_HEADS // 2)]
    for j in range(q.shape[0] // BLOCK):
        qj = q[BLOCK * j:BLOCK * (j + 1)]
        kcat = kall[BLOCK * j:BLOCK * (j + 2)]
        vcat = vall[BLOCK * j:BLOCK * (j + 2)]
        bias = (bias0_ref if j == 0 else bias_ref)[0]
        rows = pl.ds(BLOCK * r * j + c, BLOCK, stride=r) if r > 1 else slice(BLOCK * j, BLOCK * (j + 1))
        logits = jnp.concatenate(
            [_dot_nt(qj[:, sl], jnp.where(mk, kcat[:, sl], zero)) for sl in pairs for mk in masks],
            axis=0)
        logits = logits * scale + bias.reshape(A_HEADS * BLOCK, 2 * BLOCK)
        m = jnp.max(logits, axis=-1, keepdims=True)
        e = jnp.exp(logits - m)
        s = jnp.sum(e, axis=-1, keepdims=True)
        p = (e * (1.0 / s)).astype(BF16)
        lse = m + jnp.log(s)
        for pr, (sl, o_ref, l_ref) in enumerate(zip(pairs, (o0_ref, o1_ref), (l0_ref, l1_ref))):
            out = None
            for hf, mk in enumerate(masks):
                h = 2 * pr + hf
                part = _dot(p[BLOCK * h:BLOCK * (h + 1)], jnp.where(mk, vcat[:, sl], zero))
                out = part if out is None else out + part
            o_ref[rows, :] = out
            l_ref[rows, :] = jnp.where(low, lse[BLOCK * 2 * pr:BLOCK * (2 * pr + 1)],
                                       lse[BLOCK * (2 * pr + 1):BLOCK * (2 * pr + 2)])


def _attn_a_group(hr, bias, g, col_q, col_k, col_v):
    bsz, r, ln, _ = hr.shape
    nb = min(2 * ATTN_BLOCKS_PER_STEP, ln // BLOCK)
    span = nb * BLOCK
    nspan = ln // span
    half = A_GROUP_WIDTH // 2

    def spec(col, prev):
        if prev:
            return pl.BlockSpec((1, 1, BLOCK, A_GROUP_WIDTH),
                                lambda b, n, c: (b, c, jnp.maximum(n * nb - 1, 0), col))
        return pl.BlockSpec((1, 1, span, A_GROUP_WIDTH), lambda b, n, c: (b, c, n, col))

    def bias_spec(first_aware):
        return pl.BlockSpec(
            (1, A_HEADS, BLOCK, 2 * BLOCK),
            lambda b, n, c: (jnp.minimum(n, 1) if first_aware else 1, B_Q_HEADS // A_HEADS + g, 0, 0))

    out_spec = pl.BlockSpec((span * r, half), lambda b, n, c: (b * nspan + n, 0))
    return pl.pallas_call(
        functools.partial(_attn_a_kernel, r),
        grid=(bsz, nspan, r),
        in_specs=[spec(col_q, False), spec(col_k, True), spec(col_k, False),
                  spec(col_v, True), spec(col_v, False), bias_spec(True), bias_spec(False)],
        out_specs=[out_spec] * 4,
        out_shape=[jax.ShapeDtypeStruct((bsz * ln * r, half), F32)] * 4,
        compiler_params=_cparams(3, 32),
        name=f"attn_a{g}",
    )(hr, hr, hr, hr, hr, bias, bias)


def _attn_b_kernel(sinks_ref, q0_ref, q1_ref, kp_ref, kc_ref, vp_ref, vc_ref, bias0_ref, bias_ref,
                   o_ref):
    kall = jnp.concatenate([kp_ref[0], kc_ref[0]], axis=0)
    vall = jnp.concatenate([vp_ref[0], vc_ref[0]], axis=0)
    scale = HEAD_DIM ** -0.5
    per_kv = B_Q_HEADS // B_KV_HEADS
    low = lax.broadcasted_iota(jnp.int32, (1, LANES), 1) < HEAD_DIM
    zero = jnp.zeros((), BF16)
    masks = (low, jnp.logical_not(low))
    for j in range(o_ref.shape[1] // BLOCK):
        rows = slice(BLOCK * j, BLOCK * (j + 1))
        bias = bias0_ref if j == 0 else bias_ref
        for kv, q_ref in enumerate((q0_ref, q1_ref)):
            q = q_ref[0, rows, :]
            kd = kall[BLOCK * j:BLOCK * (j + 2), LANES * kv:LANES * (kv + 1)]
            vd = vall[BLOCK * j:BLOCK * (j + 2), LANES * kv:LANES * (kv + 1)]
            heads = range(per_kv * kv, per_kv * (kv + 1))
            logits = jnp.concatenate(
                [_dot_nt(q[:, LANES * (n // 2):LANES * (n // 2 + 1)], jnp.where(masks[n % 2], kd, zero))
                 for n in range(per_kv)], axis=0)
            logits = logits * scale + bias[0, per_kv * kv:per_kv * (kv + 1)].reshape(
                per_kv * BLOCK, 2 * BLOCK)
            sink = jnp.concatenate([jnp.full((BLOCK, 1), sinks_ref[h], F32) for h in heads], axis=0)
            m = jnp.maximum(jnp.max(logits, axis=-1, keepdims=True), sink)
            e = jnp.exp(logits - m)
            denom = jnp.sum(e, axis=-1, keepdims=True) + jnp.exp(sink - m)
            p = (e * (1.0 / denom)).astype(BF16)
            for pr in range(per_kv // 2):
                out = (_dot(p[BLOCK * 2 * pr:BLOCK * (2 * pr + 1)], jnp.where(masks[0], vd, zero))
                       + _dot(p[BLOCK * (2 * pr + 1):BLOCK * (2 * pr + 2)], jnp.where(masks[1], vd, zero)))
                lanes = slice(LANES * (per_kv // 2 * kv + pr), LANES * (per_kv // 2 * kv + pr + 1))
                o_ref[0, rows, lanes] = out.astype(o_ref.dtype)


def _attn_b(h3, bias, sinks, col_q, col_k, col_v):
    bsz, s, _ = h3.shape
    nb = 2 * ATTN_BLOCKS_PER_STEP
    span = nb * BLOCK
    qw = B_Q_WIDTH // B_KV_HEADS
    kvw = 2 * B_KV_WIDTH
    cq = col_q // qw
    ck = col_k // kvw
    cv = col_v // kvw

    def kv_spec(col, prev):
        if prev:
            return pl.BlockSpec((1, BLOCK, kvw), lambda b, n: (b, jnp.maximum(n * nb - 1, 0), col))
        return pl.BlockSpec((1, span, kvw), lambda b, n: (b, n, col))

    def bias_spec(first_aware):
        return pl.BlockSpec((1, B_Q_HEADS, BLOCK, 2 * BLOCK),
                            lambda b, n: (jnp.minimum(n, 1) if first_aware else 1, 0, 0, 0))

    return pl.pallas_call(
        _attn_b_kernel,
        grid=(bsz, s // span),
        in_specs=[pl.BlockSpec(memory_space=pltpu.SMEM),
                  pl.BlockSpec((1, span, qw), lambda b, n: (b, n, cq)),
                  pl.BlockSpec((1, span, qw), lambda b, n: (b, n, cq + 1)),
                  kv_spec(ck, True), kv_spec(ck, False), kv_spec(cv, True), kv_spec(cv, False),
                  bias_spec(True), bias_spec(False)],
        out_specs=pl.BlockSpec((1, span, B_Q_WIDTH), lambda b, n: (b, n, 0)),
        out_shape=jax.ShapeDtypeStruct((bsz, s, B_Q_WIDTH), BF16),
        compiler_params=_cparams(2, 32),
        name="attn_b",
    )(sinks.astype(F32), h3, h3, h3, h3, h3, h3, bias, bias)


def _attn_c_kernel(q_ref, k_ref, v_ref, o_ref):
    q = q_ref[0]
    k = k_ref[0]
    v = v_ref[0]
    scale = C_HEAD_DIM ** -0.5
    for h in range(C_HEADS):
        sl = slice(C_HEAD_DIM * h, C_HEAD_DIM * (h + 1))
        logits = _dot_nt(q[:, sl], k[:, sl]) * scale
        m = jnp.max(logits, axis=-1, keepdims=True)
        e = jnp.exp(logits - m)
        inv = 1.0 / jnp.sum(e, axis=-1, keepdims=True)
        o_ref[0, :, sl] = _dot((e * inv).astype(BF16), v[:, sl]).astype(o_ref.dtype)


def _attn_c(h3, kvm, col_q, tq):
    bsz, s, _ = h3.shape
    m = kvm.shape[1]
    cq = col_q // C_WIDTH
    return pl.pallas_call(
        _attn_c_kernel,
        grid=(bsz, s // tq),
        in_specs=[pl.BlockSpec((1, tq, C_WIDTH), lambda b, n: (b, n, cq)),
                  pl.BlockSpec((1, m, C_WIDTH), lambda b, n: (b, 0, 0)),
                  pl.BlockSpec((1, m, C_WIDTH), lambda b, n: (b, 0, 1))],
        out_specs=pl.BlockSpec((1, tq, C_WIDTH), lambda b, n: (b, n, 0)),
        out_shape=jax.ShapeDtypeStruct((bsz, s, C_WIDTH), BF16),
        compiler_params=_cparams(2, 32),
        name="attn_c",
    )(h3, kvm, kvm)


def _merge_kernel(alpha, *refs):
    group_refs, refs = refs[:4 * A_GROUPS], refs[4 * A_GROUPS:]
    (yb_ref, yc_ref, gate_ref, x_ref, bg_ref, wa_ref, wb_ref, wc_ref, wo_ref, g1_ref, b1_ref,
     wq_ref, x1_ref, x1b_ref, q_ref) = refs
    d = x_ref.shape[-1]
    halves = []
    for hf in range(2):
        o0, o1, o2 = (group_refs[4 * g + hf][...] for g in range(A_GROUPS))
        la, lb, lc = (group_refs[4 * g + 2 + hf][...] for g in range(A_GROUPS))
        m = jnp.maximum(jnp.maximum(la, lb), lc)
        wa, wb, wc = jnp.exp(la - m), jnp.exp(lb - m), jnp.exp(lc - m)
        inv = 1.0 / (wa + wb + wc)
        halves.append((wa * inv) * o0 + (wb * inv) * o1 + (wc * inv) * o2)
    ya = jnp.concatenate(halves, axis=1)

    def gate(n):
        pre = gate_ref[:, n * d:(n + 1) * d].astype(F32) + bg_ref[:, n * d:(n + 1) * d]
        return jax.nn.sigmoid(pre)

    merged = (gate(0) * _dot(ya.astype(BF16), wa_ref[...])
              + gate(1) * _dot(yb_ref[...], wb_ref[...])
              + gate(2) * _dot(yc_ref[...], wc_ref[...]))
    y = alpha * x_ref[...] + _dot(merged.astype(BF16), wo_ref[...])
    x1 = _layer_norm(y, g1_ref[...], b1_ref[...])
    x1_ref[...] = x1
    x1b = x1.astype(BF16)
    x1b_ref[...] = x1b
    q_ref[...] = _dot(x1b, wq_ref[...]).astype(q_ref.dtype)


def _merge(alpha, groups, yb, yc, h2, x2, b_gate, w_a, w_b, w_c, w_o, g1, b1, w_q, tm):
    t, d = x2.shape
    row = lambda w: pl.BlockSpec((tm, w), lambda i: (i, 0))
    full = lambda a: pl.BlockSpec(a.shape, lambda i: (0,) * a.ndim)
    weights = [b_gate, w_a, w_b, w_c, w_o, g1, b1, w_q]
    ins = [*groups, yb, yc, h2, x2, *weights]
    in_specs = ([row(A_GROUP_WIDTH // 2)] * len(groups)
                + [row(B_Q_WIDTH), row(C_WIDTH), row(N_GATES * d), row(d)]
                + [full(a) for a in weights])
    return pl.pallas_call(
        functools.partial(_merge_kernel, alpha),
        grid=(t // tm,),
        in_specs=in_specs,
        out_specs=[row(d), row(d), row(w_q.shape[1])],
        out_shape=[jax.ShapeDtypeStruct((t, d), F32), jax.ShapeDtypeStruct((t, d), BF16),
                   jax.ShapeDtypeStruct((t, w_q.shape[1]), BF16)],
        compiler_params=_cparams(1, 48),
        name="merge_ln1",
    )(*ins)


def _cmpx(a, b, need_lo=True):
    if b is None:
        return a, None
    if a is None:
        return b, None
    (va, ta), (vb, tb) = a, b
    a_first = (va > vb) | ((va == vb) & (ta < tb))
    hi = (jnp.maximum(va, vb), jnp.where(a_first, ta, tb))
    lo = (jnp.minimum(va, vb), jnp.where(a_first, tb, ta)) if need_lo else None
    return hi, lo


def _oddeven_sort_pairs(n):
    pairs = []

    def merge(lo, m, r):
        step = 2 * r
        if step < m:
            merge(lo, m, step)
            merge(lo + r, m, step)
            pairs.extend((i, i + r) for i in range(lo + r, lo + m - r, step))
        else:
            pairs.append((lo, lo + r))

    def sort(lo, m):
        if m > 1:
            sort(lo, m // 2)
            sort(lo + m // 2, m // 2)
            merge(lo, m, 1)

    sort(0, n)
    return pairs


def _sort_items(items):
    items = list(items)
    for i, j in _oddeven_sort_pairs(len(items)):
        items[i], items[j] = _cmpx(items[i], items[j])
    return items


def _bitonic_merge(seq):
    n = len(seq)
    if n == 1:
        return seq
    firsts, seconds = zip(*(_cmpx(seq[i], seq[i + n // 2]) for i in range(n // 2)))
    return _bitonic_merge(list(firsts)) + _bitonic_merge(list(seconds))


def _merge_top(a, b, k):
    p = 1
    while p < len(a) + len(b):
        p *= 2
    seq = list(a) + [None] * (p - len(a) - len(b)) + list(b)[::-1]
    while len(seq) > k:
        half = len(seq) // 2
        seq = [_cmpx(seq[i], seq[i + half], need_lo=False)[0] for i in range(half)]
    return [x for x in _bitonic_merge(seq) if x is not None][:k]


def _top_of_keys(items, k):
    runs = [_sort_items(items[i:i + k]) for i in range(0, len(items), k)]
    while len(runs) > 1:
        runs = [_merge_top(runs[i], runs[i + 1], k) for i in range(0, len(runs), 2)]
    return runs[0]


RETRIEVE_TOKENS = SUBLANES * LANES
SCORE_PITCH = N_KEYS + SUBLANES


def _tile_const(x):
    return jnp.full((SUBLANES, LANES), float(x), F32)


def _select_pairs(scores1, scores2):
    halves = [_top_of_keys([(s, _tile_const(k)) for k, s in enumerate(scores)], PEER_TOPK)
              for scores in (scores1, scores2)]
    (v1, i1), (v2, i2) = (tuple(zip(*half)) for half in halves)
    pair = lambda k1, k2: (v1[k1] + v2[k2], _tile_const(k1 * PEER_TOPK + k2))
    rows = [[pair(k1, k2) for k2 in range(PEER_TOPK // (k1 + 1))] for k1 in range(8)]
    column = [pair(k1, 0) for k1 in range(8, PEER_TOPK)]
    small = _merge_top(_merge_top(rows[5], rows[6], PEER_TOPK),
                       _merge_top(rows[7], rows[4], PEER_TOPK), PEER_TOPK)
    mid = _merge_top(_merge_top(rows[3], rows[2], PEER_TOPK), small, PEER_TOPK)
    best = _merge_top(rows[0], _merge_top(mid, _merge_top(rows[1], column, PEER_TOPK),
                                          PEER_TOPK), PEER_TOPK)
    top, pos = zip(*best)
    e = [jnp.exp(t - top[0]) for t in top]
    inv = 1.0 / functools.reduce(lambda a, b: a + b, e)
    out_i, out_j, out_g = [], [], []
    for k in range(PEER_TOPK):
        k1 = jnp.floor(pos[k] * (1.0 / PEER_TOPK))
        k2 = pos[k] - k1 * PEER_TOPK
        ei, ej = i1[0], i2[0]
        for r in range(1, PEER_TOPK):
            ei = jnp.where(k1 == r, i1[r], ei)
            ej = jnp.where(k2 == r, i2[r], ej)
        out_i.append(ei)
        out_j.append(ej)
        out_g.append(e[k] * inv)
    return out_i, out_j, out_g


def _retrieve_kernel(q_ref, keys_ref, i_ref, j_ref, g_ref, s_ref, ri_ref, rj_ref, rg_ref):
    def per_head(h, carry):
        col = pl.multiple_of(h * (2 * PEER_HALF), LANES)
        scores = []
        for c in range(2):
            keys = keys_ref[h, c]
            for ch in range(SUBLANES):
                qc = q_ref[ch * LANES:(ch + 1) * LANES, pl.ds(col, LANES)]
                s_ref[c, ch * SCORE_PITCH:ch * SCORE_PITCH + N_KEYS, :] = _dot_nt(
                    keys, qc[:, c * PEER_HALF:(c + 1) * PEER_HALF])
            scores.append([s_ref.at[c][pl.ds(k, SUBLANES, stride=SCORE_PITCH), :]
                           for k in range(N_KEYS)])
        out_i, out_j, out_g = _select_pairs(*scores)
        for k in range(PEER_TOPK):
            row0 = pl.multiple_of((h * PEER_TOPK + k) * SUBLANES, SUBLANES)
            ri_ref[pl.ds(row0, SUBLANES), :] = out_i[k]
            rj_ref[pl.ds(row0, SUBLANES), :] = out_j[k]
            rg_ref[pl.ds(row0, SUBLANES), :] = out_g[k]
        return carry

    lax.fori_loop(0, PEER_HEADS, per_head, 0)
    for ch in range(SUBLANES):
        rows = slice(ch * LANES, (ch + 1) * LANES)
        for r_ref, o_ref in ((ri_ref, i_ref), (rj_ref, j_ref), (rg_ref, g_ref)):
            o_ref[rows, :] = r_ref[pl.ds(ch, PEER_HEADS * PEER_TOPK, stride=SUBLANES), :].T


def _retrieve(q, keys):
    t, w = q.shape
    tt = RETRIEVE_TOKENS
    npair = PEER_HEADS * PEER_TOPK
    out = pl.BlockSpec((tt, npair), lambda i: (i, 0))
    slots = pltpu.VMEM((npair * SUBLANES, LANES), F32)
    return pl.pallas_call(
        _retrieve_kernel,
        grid=(t // tt,),
        in_specs=[pl.BlockSpec((tt, w), lambda i: (i, 0)),
                  pl.BlockSpec(keys.shape, lambda i: (0, 0, 0, 0))],
        out_specs=[out, out, out],
        out_shape=[jax.ShapeDtypeStruct((t, npair), F32)] * 3,
        scratch_shapes=[pltpu.VMEM((2, SUBLANES * SCORE_PITCH, LANES), F32), slots, slots, slots],
        compiler_params=_cparams(1, 32),
        name="peer_retrieve",
    )(q, keys)


GATE_TOKENS_PER_TRIP = 64


def _gate_matrix_kernel(i_ref, j_ref, g_ref, o_ref):
    tt, npair = i_ref.shape
    key = lax.broadcasted_iota(jnp.int32, (N_KEYS, npair), 0).astype(F32).astype(BF16)
    one = jnp.ones((N_KEYS, npair), BF16)
    zero = jnp.zeros((N_KEYS, npair), BF16)

    def per_trip(trip, carry):
        for u in range(GATE_TOKENS_PER_TRIP):
            t = trip * GATE_TOKENS_PER_TRIP + u
            irow = jnp.broadcast_to(i_ref[pl.ds(t, 1), :], (N_KEYS, npair)).astype(BF16)
            jrow = jnp.broadcast_to(j_ref[pl.ds(t, 1), :], (N_KEYS, npair)).astype(BF16)
            grow = jnp.broadcast_to(g_ref[pl.ds(t, 1), :], (N_KEYS, npair)).astype(BF16)
            a = jnp.where(key == irow, grow, zero)
            b = jnp.where(key == jrow, one, zero)
            grp, s = divmod(u, SUBLANES)
            start = (trip * (GATE_TOKENS_PER_TRIP // SUBLANES) + grp) * (SUBLANES * N_KEYS) + s
            o_ref[pl.ds(start, N_KEYS, stride=SUBLANES), :] = _dot_nt(a, b)
        return carry

    lax.fori_loop(0, tt // GATE_TOKENS_PER_TRIP, per_trip, 0)


def _gate_matrix(ei, ej, gate, tt):
    t, npair = ei.shape
    spec = pl.BlockSpec((tt, npair), lambda i: (i, 0))
    gm = pl.pallas_call(
        _gate_matrix_kernel,
        grid=(t // tt,),
        in_specs=[spec, spec, spec],
        out_specs=pl.BlockSpec((tt * N_KEYS, N_KEYS), lambda i: (i, 0)),
        out_shape=jax.ShapeDtypeStruct((t * N_KEYS, N_KEYS), F32),
        compiler_params=_cparams(1, 48),
        name="peer_gate_matrix",
    )(ei, ej, gate)
    return gm.reshape(t // SUBLANES, N_KEYS, SUBLANES, N_KEYS)


def _experts_kernel(alpha, sub, xb_ref, ut_ref, v_ref, gm_ref, x1_ref, g2_ref, b2_ref, o_ref, acc_ref):
    j = pl.program_id(1)

    @pl.when(j == 0)
    def _():
        acc_ref[...] = jnp.zeros_like(acc_ref)

    tt = xb_ref.shape[0]
    n_i = gm_ref.shape[1]
    for r0 in range(0, tt, sub):
        hid = _dot(xb_ref[r0:r0 + sub, :], ut_ref[...])
        act = 0.5 * hid * (1.0 + lax.erf(hid * (2.0 ** -0.5)))
        grp = slice(r0 // SUBLANES, (r0 + sub) // SUBLANES)
        gate = jnp.concatenate([gm_ref[grp, i].reshape(sub, N_KEYS) for i in range(n_i)], axis=1)
        acc_ref[r0:r0 + sub, :] += _dot((gate * act).astype(BF16), v_ref[...])

    @pl.when(j == pl.num_programs(1) - 1)
    def _():
        y = alpha * x1_ref[...] + acc_ref[...]
        o_ref[...] = _layer_norm(y, g2_ref[...], b2_ref[...])


def _experts(alpha, x1b, ut, v, gm, x1, g2, b2, tt, ec, sub):
    t, d = x1.shape
    ne = ut.shape[1]
    return pl.pallas_call(
        functools.partial(_experts_kernel, alpha, sub),
        grid=(t // tt, ne // ec),
        in_specs=[pl.BlockSpec((tt, d), lambda i, j: (i, 0)),
                  pl.BlockSpec((d, ec), lambda i, j: (0, j)),
                  pl.BlockSpec((ec, d), lambda i, j: (j, 0)),
                  pl.BlockSpec((tt // SUBLANES, ec // N_KEYS, SUBLANES, N_KEYS),
                               lambda i, j: (i, j, 0, 0)),
                  pl.BlockSpec((tt, d), lambda i, j: (i, 0)),
                  pl.BlockSpec((1, d), lambda i, j: (0, 0)),
                  pl.BlockSpec((1, d), lambda i, j: (0, 0))],
        out_specs=pl.BlockSpec((tt, d), lambda i, j: (i, 0)),
        out_shape=jax.ShapeDtypeStruct((t, d), F32),
        scratch_shapes=[pltpu.VMEM((tt, d), F32)],
        compiler_params=_cparams(2, 56),
        name="peer_experts",
    )(x1b, ut, v, gm, x1, g2, b2)


def _layer(x, mem, rel_bias, w_in, b_gate, w_mem_kv, sinks, w_a, w_b, w_c, w_out, g1, b1,
           w_query, sub_keys, u_tab, v_tab, g2, b2, alpha):
    bsz, s, d = x.shape
    t = bsz * s

    def a_cols(g):
        return [w_in[:, part * A_WIDTH + g * A_GROUP_WIDTH:part * A_WIDTH + (g + 1) * A_GROUP_WIDTH]
                for part in range(3)]

    def twice_per_kv_head(start):
        heads = [w_in[:, start + HEAD_DIM * kv:start + HEAD_DIM * (kv + 1)] for kv in range(B_KV_HEADS)]
        return [w for head in heads for w in (head, head)]

    bq_at = 3 * A_WIDTH
    bk_at = bq_at + B_Q_WIDTH
    bv_at = bk_at + B_KV_WIDTH
    cq_at = bv_at + B_KV_WIDTH
    gates_at = cq_at + C_WIDTH
    w_nat = jnp.concatenate([w_in[:, gates_at:], w_in[:, cq_at:gates_at], *a_cols(0),
                             w_in[:, bq_at:bk_at], *twice_per_kv_head(bk_at),
                             *twice_per_kv_head(bv_at)], axis=1).astype(BF16)
    dils = [dil for _, dil in A_PAIRS[1:]]
    w_dil = [jnp.concatenate(a_cols(g), axis=1).astype(BF16) for g in range(1, A_GROUPS)]
    row_w = w_nat.shape[1]
    col_cq = N_GATES * d
    col_aq = col_cq + C_WIDTH
    col_bq = col_aq + 3 * A_GROUP_WIDTH
    col_bk = col_bq + B_Q_WIDTH
    col_bv = col_bk + 2 * B_KV_WIDTH

    x2 = x.reshape(t, d)
    h2, *h_dil = _in_proj(x2, bsz, w_nat, w_dil, dils, 256)
    h3 = h2.reshape(bsz, s, row_w)

    bias = _bias_tiles(rel_bias)
    groups = []
    for g in range(A_GROUPS):
        if g == 0:
            hr, c0 = h3.reshape(bsz, 1, s, row_w), col_aq // A_GROUP_WIDTH
        else:
            hr, c0 = h_dil[g - 1], 0
        groups += _attn_a_group(hr, bias, g, c0, c0 + 1, c0 + 2)

    yb = _attn_b(h3, bias, sinks, col_bq, col_bk, col_bv).reshape(t, B_Q_WIDTH)

    m = mem.shape[1]
    kvm = _project(mem.reshape(bsz * m, d), w_mem_kv.astype(BF16), 256, 2 * C_WIDTH, "mem_kv")
    yc = _attn_c(h3, kvm.reshape(bsz, m, 2 * C_WIDTH), col_cq, 512).reshape(t, C_WIDTH)

    x1, x1b, q = _merge(alpha, groups, yb, yc, h2, x2, b_gate.reshape(1, -1).astype(F32),
                        w_a.astype(BF16), w_b.astype(BF16), w_c.astype(BF16), w_out.astype(BF16),
                        g1.reshape(1, d).astype(F32), b1.reshape(1, d).astype(F32),
                        w_query.astype(BF16), 512)

    ei, ej, gate = _retrieve(q, sub_keys.astype(BF16))
    gm = _gate_matrix(ei, ej, gate, 256)
    out = _experts(alpha, x1b, u_tab.astype(BF16).T, v_tab.astype(BF16), gm, x1,
                   g2.reshape(1, d).astype(F32), b2.reshape(1, d).astype(F32), 1024, 1024, 256)
    return out.reshape(bsz, s, d)


def kernel(x, mem, rel_bias, w_in, b_gate, w_mem_kv, sinks, w_branch_a, w_branch_b, w_branch_c,
           w_out, ln1_g, ln1_b, peer_w_query, peer_sub_keys, peer_u, peer_v, ln2_g, ln2_b):
    depth = w_in.shape[0]
    alpha = (2.0 * depth) ** 0.25
    for l in range(depth):
        x = _layer(x, mem, rel_bias, w_in[l], b_gate[l], w_mem_kv[l], sinks[l], w_branch_a[l],
                   w_branch_b[l], w_branch_c[l], w_out[l], ln1_g[l], ln1_b[l], peer_w_query[l],
                   peer_sub_keys[l], peer_u[l], peer_v[l], ln2_g[l], ln2_b[l], alpha)
    return x
```
